```python
import jax, jax.numpy as jnp
from jax import lax
import numpy as np

D_MODEL = 2048
BATCH = 1
SEQ = 8192
DEPTH = 1
DEC_BATCH = 32
DEC_SEQ = 1
PAST_LEN = 16384
PAGE_SIZE = 128

N_META = 16
BLOCK = 128
EPS = 1e-6
F_HEADS = 8
F_HDIM = 128
F_WIDTH = F_HEADS * F_HDIM
M_HEADS = 4
M_VDIM = 256
M_QKDIM = 128
M_WIDTH = M_HEADS * M_VDIM
M_QKW = M_HEADS * M_QKDIM
MIX_WIDTH = F_WIDTH + M_WIDTH
PROJ_SIZES = (F_WIDTH, F_WIDTH, F_WIDTH, F_HEADS, F_WIDTH,
              M_QKW, M_QKW, M_WIDTH, M_HEADS, M_HEADS, M_WIDTH, M_WIDTH)
D_PROJ = 4 * F_WIDTH + F_HEADS + 2 * M_QKW + 3 * M_WIDTH + 2 * M_HEADS

kernel_name = 'fox_mlstm_hybrid_step'

F32 = jnp.float32


def rmsnorm(x, g):
    xf = x.astype(F32)
    y = xf * lax.rsqrt(jnp.mean(xf * xf, axis=-1, keepdims=True) + EPS) * g.astype(F32)
    return y.astype(x.dtype)


def in_proj(xn, w_in, b_fox_f, b_m_i, b_m_f):
    B, T = xn.shape[:2]
    p = jnp.einsum('btd,dp->btp', xn, w_in)
    idx = np.cumsum(PROJ_SIZES)[:-1].tolist()
    fq, fk, fv, ff, fz, mq, mk, mv, mi, mf, mo, mz = jnp.split(p, idx, axis=-1)
    fh = lambda a: a.reshape(B, T, F_HEADS, F_HDIM)
    f_logf = jax.nn.log_sigmoid((ff + b_fox_f).astype(F32))
    m_logi = (mi + b_m_i).astype(F32)
    m_logf = jax.nn.log_sigmoid((mf + b_m_f).astype(F32))
    fox = (fh(fq), fh(fk), fh(fv), f_logf, fz)
    mls = (mq.reshape(B, T, M_HEADS, M_QKDIM), mk.reshape(B, T, M_HEADS, M_QKDIM),
           mv.reshape(B, T, M_HEADS, M_VDIM), m_logi, m_logf, mo, mz)
    return fox, mls


def fox_prompt(q, k, v, logf):
    B, L = q.shape[:2]
    scale = F_HDIM ** -0.5
    c = jnp.cumsum(logf, axis=1)
    pos = jnp.arange(L)

    def attend(qb, cq, qpos, kk, vv, ck, kpos):
        s = jnp.einsum('bqhd,bkhd->bhqk', qb, kk, preferred_element_type=F32) * scale
        s = s + jnp.swapaxes(cq, 1, 2)[..., :, None] - jnp.swapaxes(ck, 1, 2)[..., None, :]
        s = jnp.where(kpos[None, None, None, :] <= qpos[None, None, :, None], s, -jnp.inf)
        p = jax.nn.softmax(s, axis=-1).astype(vv.dtype)
        return jnp.einsum('bhqk,bkhd->bqhd', p, vv)

    meta_out = attend(q[:, :N_META], c[:, :N_META], pos[:N_META],
                      k[:, :N_META], v[:, :N_META], c[:, :N_META], pos[:N_META])
    n_blk = (L - N_META) // BLOCK

    def blk(i):
        start = N_META + i * BLOCK
        qb = lax.dynamic_slice_in_dim(q, start, BLOCK, axis=1)
        cq = lax.dynamic_slice_in_dim(c, start, BLOCK, axis=1)
        return attend(qb, cq, start + jnp.arange(BLOCK), k, v, c, pos)

    outs = lax.map(blk, jnp.arange(n_blk))
    real = jnp.moveaxis(outs, 0, 1).reshape(B, n_blk * BLOCK, F_HEADS, F_HDIM)
    return jnp.concatenate([meta_out, real], axis=1)


def fox_sample(q, k_new, v_new, logf_new, cache_k, cache_v, cache_logf, page_table):
    scale = F_HDIM ** -0.5

    def one(args):
        qs, kn, vn, lfn, pt = args
        kp = cache_k[pt].reshape(-1, F_HEADS, F_HDIM)
        vp = cache_v[pt].reshape(-1, F_HEADS, F_HDIM)
        lfp = cache_logf[pt].reshape(-1, F_HEADS).astype(F32)
        P = kp.shape[0]
        T = qs.shape[0]
        suffix = lax.cumsum(lfp, axis=0, reverse=True) - lfp
        cn = jnp.cumsum(lfn, axis=0)
        s_past = (jnp.einsum('qhd,khd->hqk', qs, kp, preferred_element_type=F32) * scale
                  + cn.T[:, :, None] + suffix.T[:, None, :])
        s_new = (jnp.einsum('qhd,khd->hqk', qs, kn, preferred_element_type=F32) * scale
                 + cn.T[:, :, None] - cn.T[:, None, :])
        causal = jnp.tril(jnp.ones((T, T), dtype=bool))
        s_new = jnp.where(causal[None], s_new, -jnp.inf)
        p = jax.nn.softmax(jnp.concatenate([s_past, s_new], axis=-1), axis=-1).astype(vn.dtype)
        return (jnp.einsum('hqk,khd->qhd', p[..., :P], vp)
                + jnp.einsum('hqk,khd->qhd', p[..., P:], vn))

    return lax.map(one, (q, k_new, v_new, logf_new, page_table))


def mlstm_chunk(state, q, k, v, logi, logf):
    C, n, m = state
    q = q.astype(F32)
    k = k.astype(F32) * (M_QKDIM ** -0.5)
    v = v.astype(F32)
    L = q.shape[1]
    bt = jnp.swapaxes(jnp.cumsum(logf, axis=1), 1, 2)
    it = jnp.swapaxes(logi, 1, 2)
    causal = jnp.tril(jnp.ones((L, L), dtype=bool))
    dlog = jnp.where(causal, bt[..., :, None] - bt[..., None, :] + it[..., None, :], -jnp.inf)
    inter = bt + m[..., None]
    m_t = jnp.maximum(inter, jnp.max(dlog, axis=-1))
    dw = jnp.exp(dlog - m_t[..., None])
    iw = jnp.exp(inter - m_t)
    s = jnp.einsum('bthd,bshd->bhts', q, k) * dw
    num = jnp.einsum('bhts,bshv->bhtv', s, v) + iw[..., None] * jnp.einsum('bhvd,bthd->bhtv', C, q)
    den = jnp.sum(s, axis=-1) + iw * jnp.einsum('bhd,bthd->bht', n, q)
    h = num / jnp.maximum(jnp.abs(den), jnp.exp(-m_t))[..., None]
    m_new = m_t[..., -1]
    wl = jnp.exp(bt[..., -1:] - bt + it - m_new[..., None])
    decay = jnp.exp(inter[..., -1] - m_new)
    C_new = decay[..., None, None] * C + jnp.einsum('bhs,bshv,bshd->bhvd', wl, v, k)
    n_new = decay[..., None] * n + jnp.einsum('bhs,bshd->bhd', wl, k)
    return (C_new, n_new, m_new), jnp.swapaxes(h, 1, 2)


def mlstm_prompt(q, k, v, logi, logf):
    B, L = q.shape[:2]
    state0 = (jnp.zeros((B, M_HEADS, M_VDIM, M_QKDIM), F32),
              jnp.zeros((B, M_HEADS, M_QKDIM), F32),
              jnp.zeros((B, M_HEADS), F32))
    state, h_meta = mlstm_chunk(state0, q[:, :N_META], k[:, :N_META], v[:, :N_META],
                                logi[:, :N_META], logf[:, :N_META])
    n_chunk = (L - N_META) // BLOCK

    def to_chunks(a):
        return jnp.moveaxis(a[:, N_META:].reshape((B, n_chunk, BLOCK) + a.shape[2:]), 1, 0)

    def step(st, xs):
        return mlstm_chunk(st, *xs)

    state, hs = lax.scan(step, state, (to_chunks(q), to_chunks(k), to_chunks(v),
                                       to_chunks(logi), to_chunks(logf)))
    h_real = jnp.moveaxis(hs, 0, 1).reshape(B, n_chunk * BLOCK, M_HEADS, M_VDIM)
    return jnp.concatenate([h_meta, h_real], axis=1), state


def mix_out(fox_o, fz, m_h, mo, mz, mh_gain, w_out, dtype):
    B, T = fz.shape[:2]
    f_part = jax.nn.silu(fz.astype(F32)) * fox_o.reshape(B, T, F_WIDTH).astype(F32)
    mh = m_h * lax.rsqrt(jnp.mean(m_h * m_h, axis=-1, keepdims=True) + EPS)
    mh = mh.reshape(B, T, M_WIDTH) * mh_gain.astype(F32)
    m_part = jax.nn.silu(mz.astype(F32)) * jax.nn.sigmoid(mo.astype(F32)) * mh
    cat = jnp.concatenate([f_part, m_part], axis=-1).astype(dtype)
    return jnp.einsum('btc,cd->btd', cat, w_out)


def setup_inputs(seed: int = 0) -> dict:
    key = jax.random.key(seed)
    ks = jax.random.split(key, 20)
    n_pages = PAST_LEN // PAGE_SIZE
    n_pool = (5 * DEC_BATCH * n_pages) // 4
    nrm = jax.random.normal
    x_prompt = nrm(ks[0], (BATCH, SEQ, D_MODEL), F32)
    x_sample = nrm(ks[1], (DEC_BATCH, DEC_SEQ, D_MODEL), F32)
    cache_k = nrm(ks[2], (DEPTH, n_pool, PAGE_SIZE, F_HEADS, F_HDIM), F32)
    cache_v = nrm(ks[3], (DEPTH, n_pool, PAGE_SIZE, F_HEADS, F_HDIM), F32)
    cache_logf = jax.nn.log_sigmoid(8.0 + 0.5 * nrm(ks[4], (DEPTH, n_pool, PAGE_SIZE, F_HEADS), F32))
    state_C = 0.3 * nrm(ks[5], (DEPTH, DEC_BATCH, M_HEADS, M_VDIM, M_QKDIM), F32)
    state_n = 0.3 * nrm(ks[6], (DEPTH, DEC_BATCH, M_HEADS, M_QKDIM), F32)
    state_m = nrm(ks[7], (DEPTH, DEC_BATCH, M_HEADS), F32)
    page_table = jax.random.permutation(ks[8], n_pool)[:DEC_BATCH * n_pages]
    page_table = page_table.reshape(DEC_BATCH, n_pages).astype(jnp.int32)
    meta_tokens = nrm(ks[9], (N_META, D_MODEL), F32)
    norm_gain = 1.0 + 0.02 * nrm(ks[10], (DEPTH, D_MODEL), F32)
    w_in = nrm(ks[11], (DEPTH, D_MODEL, D_PROJ), F32) * (D_MODEL ** -0.5)
    b_fox_f = 2.0 + 0.5 * nrm(ks[12], (DEPTH, F_HEADS), F32)
    b_m_i = 0.1 * nrm(ks[13], (DEPTH, M_HEADS), F32)
    b_m_f = 3.0 + 0.5 * nrm(ks[14], (DEPTH, M_HEADS), F32)
    mh_gain = 1.0 + 0.02 * nrm(ks[15], (DEPTH, M_WIDTH), F32)
    w_out = nrm(ks[16], (DEPTH, MIX_WIDTH, D_MODEL), F32) * (MIX_WIDTH ** -0.5)
    final_gain = 1.0 + 0.02 * nrm(ks[17], (D_MODEL,), F32)
    return {'x_prompt': x_prompt, 'x_sample': x_sample, 'cache_k': cache_k, 'cache_v': cache_v,
            'cache_logf': cache_logf, 'state_C': state_C, 'state_n': state_n, 'state_m': state_m,
            'page_table': page_table, 'meta_tokens': meta_tokens, 'norm_gain': norm_gain,
            'w_in': w_in, 'b_fox_f': b_fox_f, 'b_m_i': b_m_i, 'b_m_f': b_m_f,
            'mh_gain': mh_gain, 'w_out': w_out, 'final_gain': final_gain}


def reference(x_prompt, x_sample, cache_k, cache_v, cache_logf, state_C, state_n, state_m,
              page_table, meta_tokens, norm_gain, w_in, b_fox_f, b_m_i, b_m_f, mh_gain,
              w_out, final_gain):
    B = x_prompt.shape[0]
    meta = jnp.broadcast_to(meta_tokens.astype(x_prompt.dtype)[None], (B, N_META, D_MODEL))
    hp = jnp.concatenate([meta, x_prompt], axis=1)
    hs = x_sample
    kp_l, vp_l, lp_l, Cp_l, np_l, mp_l = [], [], [], [], [], []
    ks_l, vs_l, ls_l, Cs_l, ns_l, ms_l = [], [], [], [], [], []
    for l in range(DEPTH):
        xn = rmsnorm(hp, norm_gain[l])
        (fq, fk, fv, flogf, fz), (mq, mk, mv, mlogi, mlogf, mo, mz) = in_proj(
            xn, w_in[l], b_fox_f[l], b_m_i[l], b_m_f[l])
        fo = fox_prompt(fq, fk, fv, flogf)
        mh, (Cp, n_p, m_p) = mlstm_prompt(mq, mk, mv, mlogi, mlogf)
        hp = hp + mix_out(fo, fz, mh, mo, mz, mh_gain[l], w_out[l], hp.dtype)
        kp_l.append(fk); vp_l.append(fv); lp_l.append(flogf.astype(cache_logf.dtype))
        Cp_l.append(Cp.astype(state_C.dtype)); np_l.append(n_p.astype(state_n.dtype)); mp_l.append(m_p.astype(state_m.dtype))
        xs = rmsnorm(hs, norm_gain[l])
        (sq, sk, sv, slogf, sz), (tq, tk, tv, tlogi, tlogf, to, tz) = in_proj(
            xs, w_in[l], b_fox_f[l], b_m_i[l], b_m_f[l])
        so = fox_sample(sq, sk, sv, slogf, cache_k[l], cache_v[l], cache_logf[l], page_table)
        st0 = (state_C[l].astype(F32), state_n[l].astype(F32), state_m[l].astype(F32))
        (Cs, n_s, m_s), th = mlstm_chunk(st0, tq, tk, tv, tlogi, tlogf)
        hs = hs + mix_out(so, sz, th, to, tz, mh_gain[l], w_out[l], hs.dtype)
        ks_l.append(sk); vs_l.append(sv); ls_l.append(slogf.astype(cache_logf.dtype))
        Cs_l.append(Cs.astype(state_C.dtype)); ns_l.append(n_s.astype(state_n.dtype)); ms_l.append(m_s.astype(state_m.dtype))
    y_prompt = rmsnorm(hp[:, N_META:], final_gain)
    y_sample = rmsnorm(hs, final_gain)
    return (y_prompt, y_sample,
            jnp.stack(kp_l), jnp.stack(vp_l), jnp.stack(lp_l),
            jnp.stack(Cp_l), jnp.stack(np_l), jnp.stack(mp_l),
            jnp.stack(ks_l), jnp.stack(vs_l), jnp.stack(ls_l),
            jnp.stack(Cs_l), jnp.stack(ns_l), jnp.stack(ms_l))
```

```python
import functools

import numpy as np
import jax
import jax.numpy as jnp
from jax import lax
from jax.experimental import pallas as pl
from jax.experimental.pallas import tpu as pltpu

F32 = jnp.float32
BF16 = jnp.bfloat16

N_META = 16
EPS = 1e-6
F_HEADS = 8
F_HDIM = 128
F_WIDTH = F_HEADS * F_HDIM
M_HEADS = 4
M_VDIM = 256
M_QKDIM = 128
M_WIDTH = M_HEADS * M_VDIM
M_QKW = M_HEADS * M_QKDIM
PROJ_SIZES = (F_WIDTH, F_WIDTH, F_WIDTH, F_HEADS, F_WIDTH,
              M_QKW, M_QKW, M_WIDTH, M_HEADS, M_HEADS, M_WIDTH, M_WIDTH)

LANES = 128
CHUNK = 128
COLB = 1024
N_COLB = 8
G_FOX = 0
G_MF = 8
G_MI = 12
NEG = -1e30
VMEM_LIMIT = 56 * 1024 * 1024

_NT = (((1,), (1,)), ((), ()))
_TN = (((0,), (0,)), ((), ()))


def _cparams(sem, vmem=VMEM_LIMIT):
    return pltpu.CompilerParams(dimension_semantics=sem, vmem_limit_bytes=vmem)


def _pick_tile(n, cands):
    for c in cands:
        if n % c == 0:
            return c
    raise ValueError(f"no tile for {n}")


def _log_sigmoid(x):
    return jnp.minimum(x, 0.0) - jnp.log1p(jnp.exp(-jnp.abs(x)))


def _sigmoid(x):
    return 1.0 / (1.0 + jnp.exp(-x))


def _silu(x):
    return x * _sigmoid(x)


def _split3(x):
    x1 = x.astype(BF16)
    r1 = x - x1.astype(F32)
    x2 = r1.astype(BF16)
    r2 = r1 - x2.astype(F32)
    x3 = r2.astype(BF16)
    return x1, x2, x3


def _dot3_l(a_bf, x):
    x1, x2, x3 = _split3(x)
    d = lambda v: jnp.dot(a_bf, v, preferred_element_type=F32)
    return (d(x3) + d(x2)) + d(x1)


def _dot3_r(x, b_bf):
    x1, x2, x3 = _split3(x)
    d = lambda v: jnp.dot(v, b_bf, preferred_element_type=F32)
    return (d(x3) + d(x2)) + d(x1)


def _norm_kernel(x_ref, prev_ref, meta_ref, g_ref, xs_ref, xn_ref, xsn_ref, *, tile, n_real):
    i = pl.program_id(0)
    g = g_ref[...]

    def nrm(x):
        return x * lax.rsqrt(jnp.mean(x * x, axis=-1, keepdims=True) + EPS) * g

    top = jnp.where(i == 0, meta_ref[...], prev_ref[...])
    xn_ref[0:N_META, :] = nrm(top).astype(BF16)
    body = x_ref[0:tile - N_META, :]
    row = i * tile + N_META + lax.broadcasted_iota(jnp.int32, (tile - N_META, 1), 0)
    xn_ref[N_META:tile, :] = jnp.where(row < n_real, nrm(body), 0.0).astype(BF16)

    @pl.when(i == 0)
    def _():
        xsn_ref[...] = nrm(xs_ref[...]).astype(BF16)


def _norm_call(x, meta, gain, xs, lp, tile):
    seq, d = x.shape
    n_real = seq + N_META
    nblk = pl.cdiv(seq, tile)
    per = tile // N_META
    nprev = seq // N_META
    s = xs.shape[0]
    return pl.pallas_call(
        functools.partial(_norm_kernel, tile=tile, n_real=n_real),
        grid=(lp // tile,),
        in_specs=[
            pl.BlockSpec((tile, d), lambda i: (jnp.minimum(i, nblk - 1), 0)),
            pl.BlockSpec((N_META, d), lambda i: (jnp.clip(i * per - 1, 0, nprev - 1), 0)),
            pl.BlockSpec((N_META, d), lambda i: (0, 0)),
            pl.BlockSpec((1, d), lambda i: (0, 0)),
            pl.BlockSpec((s, d), lambda i: (0, 0)),
        ],
        out_specs=[
            pl.BlockSpec((tile, d), lambda i: (i, 0)),
            pl.BlockSpec((s, d), lambda i: (0, 0)),
        ],
        out_shape=[jax.ShapeDtypeStruct((lp, d), BF16), jax.ShapeDtypeStruct((s, d), BF16)],
        compiler_params=_cparams(("arbitrary",)),
        name="norm",
    )(x, x, meta, gain, xs)


def _inproj_kernel(x_ref, w_ref, wg_ref, xs_ref,
                   q_ref, ko_ref, kb_ref, vo_ref, vb_ref, p2_ref, g_ref, ps_ref, gs_ref):
    n = pl.program_id(0)
    m = pl.program_id(1)
    acc = jnp.dot(x_ref[...], w_ref[...], preferred_element_type=F32)

    @pl.when(n == 0)
    def _():
        q_ref[...] = acc.astype(BF16)
        g_ref[...] = jnp.dot(x_ref[...], wg_ref[...], preferred_element_type=F32)

    @pl.when(n == 1)
    def _():
        ko_ref[...] = acc
        kb_ref[...] = acc.astype(BF16)

    @pl.when(n == 2)
    def _():
        vo_ref[...] = acc
        vb_ref[...] = acc.astype(BF16)

    @pl.when(n >= 3)
    def _():
        p2_ref[...] = acc

    @pl.when(m == 0)
    def _():
        ps_ref[...] = jnp.dot(xs_ref[...], w_ref[...], preferred_element_type=F32)

    @pl.when((m == 0) & (n == 0))
    def _():
        gs_ref[...] = jnp.dot(xs_ref[...], wg_ref[...], preferred_element_type=F32)


def _inproj_call(xn, w_main, w_gate, xsn, n_real, tile):
    lp, d = xn.shape
    s = xsn.shape[0]
    nm = lp // tile

    def own(lo, hi):
        return lambda n, m: (jnp.where(n < lo, 0, jnp.where(n > hi, nm - 1, m)), 0)

    return pl.pallas_call(
        _inproj_kernel,
        grid=(N_COLB, nm),
        in_specs=[
            pl.BlockSpec((tile, d), lambda n, m: (m, 0)),
            pl.BlockSpec((d, COLB), lambda n, m: (0, n)),
            pl.BlockSpec((d, LANES), lambda n, m: (0, 0)),
            pl.BlockSpec((s, d), lambda n, m: (0, 0)),
        ],
        out_specs=[
            pl.BlockSpec((tile, COLB), own(0, 0)),
            pl.BlockSpec((tile, COLB), own(1, 1)),
            pl.BlockSpec((tile, COLB), own(1, 1)),
            pl.BlockSpec((tile, COLB), own(2, 2)),
            pl.BlockSpec((tile, COLB), own(2, 2)),
            pl.BlockSpec((tile, COLB), lambda n, m: (jnp.where(n < 3, 0, m), jnp.maximum(n - 3, 0))),
            pl.BlockSpec((tile, LANES), own(0, 0)),
            pl.BlockSpec((s, COLB), lambda n, m: (0, n)),
            pl.BlockSpec((s, LANES), lambda n, m: (0, 0)),
        ],
        out_shape=[
            jax.ShapeDtypeStruct((lp, COLB), BF16),
            jax.ShapeDtypeStruct((n_real, COLB), F32),
            jax.ShapeDtypeStruct((lp, COLB), BF16),
            jax.ShapeDtypeStruct((n_real, COLB), F32),
            jax.ShapeDtypeStruct((lp, COLB), BF16),
            jax.ShapeDtypeStruct((lp, (N_COLB - 3) * COLB), F32),
            jax.ShapeDtypeStruct((lp, LANES), F32),
            jax.ShapeDtypeStruct((s, N_COLB * COLB), F32),
            jax.ShapeDtypeStruct((s, LANES), F32),
        ],
        compiler_params=_cparams(("arbitrary", "arbitrary")),
        name="in_proj",
    )(xn, w_main, w_gate, xsn)


def _gates_kernel(g_ref, b_ref, ltri_ref, gc_ref, ct_ref, lf_ref, carry_ref, *, n_real):
    i = pl.program_id(0)

    @pl.when(i == 0)
    def _():
        carry_ref[...] = jnp.zeros_like(carry_ref)

    x = g_ref[...] + b_ref[...]
    lane = lax.broadcasted_iota(jnp.int32, (CHUNK, LANES), 1)
    row = i * CHUNK + lax.broadcasted_iota(jnp.int32, (CHUNK, LANES), 0)
    valid = row < n_real
    ls = _log_sigmoid(x)
    summed = jnp.where(valid & (lane < G_MI), ls, 0.0)
    cs = _dot3_l(ltri_ref[...], summed)
    glob = cs + carry_ref[...]
    out = jnp.where(lane < G_MF, glob,
                    jnp.where(lane < G_MI, cs, jnp.where(valid, x, NEG)))
    gc_ref[...] = out
    ct_ref[...] = out.T[0:16, :]
    lf_ref[...] = ls[:, 0:F_HEADS]
    carry_ref[...] = glob[CHUNK - 1:CHUNK, :]


def _gates_call(gates, bias_row, n_real):
    lp = gates.shape[0]
    nb = lp // CHUNK
    ltri = jnp.asarray(np.tril(np.ones((CHUNK, CHUNK), np.float32)), BF16)
    return pl.pallas_call(
        functools.partial(_gates_kernel, n_real=n_real),
        grid=(nb,),
        in_specs=[
            pl.BlockSpec((CHUNK, LANES), lambda i: (i, 0)),
            pl.BlockSpec((1, LANES), lambda i: (0, 0)),
            pl.BlockSpec((CHUNK, CHUNK), lambda i: (0, 0)),
        ],
        out_specs=[
            pl.BlockSpec((CHUNK, LANES), lambda i: (i, 0)),
            pl.BlockSpec((16, CHUNK), lambda i: (0, i)),
            pl.BlockSpec((CHUNK, F_HEADS), lambda i: (i, 0)),
        ],
        out_shape=[
            jax.ShapeDtypeStruct((lp, LANES), F32),
            jax.ShapeDtypeStruct((16, lp), F32),
            jax.ShapeDtypeStruct((n_real, F_HEADS), F32),
        ],
        scratch_shapes=[pltpu.VMEM((1, LANES), F32)],
        compiler_params=_cparams(("arbitrary",)),
        name="gates",
    )(gates, bias_row, ltri)


def _fox_kernel(q_ref, k_ref, v_ref, ct_ref, gc_ref, fz_ref, o_ref, acc_ref, m_ref, l_ref, *, scale):
    qi = pl.program_id(0)
    ki = pl.program_id(1)
    tq = q_ref.shape[0]
    tk = k_ref.shape[0]

    @pl.when(ki == 0)
    def _():
        m_ref[...] = jnp.full_like(m_ref, -jnp.inf)
        l_ref[...] = jnp.zeros_like(l_ref)
        acc_ref[...] = jnp.zeros_like(acc_ref)

    def step(masked):
        if masked:
            keep = (lax.broadcasted_iota(jnp.int32, (tq, tk), 0)
                    >= lax.broadcasted_iota(jnp.int32, (tq, tk), 1))
        for h in range(F_HEADS):
            hs = slice(h * F_HDIM, (h + 1) * F_HDIM)
            s = lax.dot_general(q_ref[:, hs], k_ref[:, hs], _NT, preferred_element_type=F32)
            t = s * scale - ct_ref[G_FOX + h:G_FOX + h + 1, :]
            if masked:
                t = jnp.where(keep, t, -jnp.inf)
            cq = gc_ref[:, G_FOX + h:G_FOX + h + 1]
            m_prev = m_ref[h]
            m_new = jnp.maximum(m_prev, jnp.max(t, axis=1, keepdims=True) + cq)
            p = jnp.exp(t - (m_new - cq))
            alpha = jnp.exp(m_prev - m_new)
            l_ref[h] = alpha * l_ref[h] + jnp.sum(p, axis=1, keepdims=True)
            acc_ref[:, hs] = alpha * acc_ref[:, hs] + jnp.dot(
                p.astype(BF16), v_ref[:, hs], preferred_element_type=F32)
            m_ref[h] = m_new

    @pl.when(ki < qi)
    def _():
        step(False)

    @pl.when(ki == qi)
    def _():
        step(True)
        for h in range(F_HEADS):
            hs = slice(h * F_HDIM, (h + 1) * F_HDIM)
            o = acc_ref[:, hs] / l_ref[h]
            o_ref[:, hs] = (_silu(fz_ref[:, hs]) * o).astype(BF16)


def _fox_call(q, k, v, ct, gc, p2, tile):
    lp = q.shape[0]
    nq = lp // tile
    kmap = lambda qi, ki: (jnp.minimum(ki, qi), 0)
    return pl.pallas_call(
        functools.partial(_fox_kernel, scale=F_HDIM ** -0.5),
        grid=(nq, nq),
        in_specs=[
            pl.BlockSpec((tile, F_WIDTH), lambda qi, ki: (qi, 0)),
            pl.BlockSpec((tile, F_WIDTH), kmap),
            pl.BlockSpec((tile, F_WIDTH), kmap),
            pl.BlockSpec((16, tile), lambda qi, ki: (0, jnp.minimum(ki, qi))),
            pl.BlockSpec((tile, LANES), lambda qi, ki: (qi, 0)),
            pl.BlockSpec((tile, COLB), lambda qi, ki: (qi, 0)),
        ],
        out_specs=pl.BlockSpec((tile, F_WIDTH), lambda qi, ki: (qi, 0)),
        out_shape=jax.ShapeDtypeStruct((lp, F_WIDTH), BF16),
        scratch_shapes=[
            pltpu.VMEM((tile, F_WIDTH), F32),
            pltpu.VMEM((F_HEADS, tile, 1), F32),
            pltpu.VMEM((F_HEADS, tile, 1), F32),
        ],
        compiler_params=_cparams(("arbitrary", "arbitrary")),
        name="fox",
    )(q, k, v, ct, gc, p2)


def _mlstm_kernel(qk_ref, v_ref, o_ref, z_ref, gc_ref, ct_ref, gain_ref,
                  mp_ref, c_out, n_out, m_out, c_s, n_s, m_s):
    i = pl.program_id(0)

    @pl.when(i == 0)
    def _():
        c_s[...] = jnp.zeros_like(c_s)
        n_s[...] = jnp.zeros_like(n_s)
        m_s[...] = jnp.zeros_like(m_s)

    causal = (lax.broadcasted_iota(jnp.int32, (CHUNK, CHUNK), 0)
              >= lax.broadcasted_iota(jnp.int32, (CHUNK, CHUNK), 1))
    for h in range(M_HEADS):
        qs = slice(h * M_QKDIM, (h + 1) * M_QKDIM)
        ks = slice(M_QKW + h * M_QKDIM, M_QKW + (h + 1) * M_QKDIM)
        vs = slice(h * M_VDIM, (h + 1) * M_VDIM)
        q = qk_ref[:, qs]
        k = qk_ref[:, ks] * (M_QKDIM ** -0.5)
        v = v_ref[:, vs]
        bt_c = gc_ref[:, G_MF + h:G_MF + h + 1]
        it_c = gc_ref[:, G_MI + h:G_MI + h + 1]
        bt_r = ct_ref[G_MF + h:G_MF + h + 1, :]
        it_r = ct_ref[G_MI + h:G_MI + h + 1, :]
        m_prev = m_s[h][:, 0:1]
        dlog = jnp.where(causal, bt_c - bt_r + it_r, -jnp.inf)
        inter = bt_c + m_prev
        m_t = jnp.maximum(inter, jnp.max(dlog, axis=1, keepdims=True))
        dw = jnp.exp(dlog - m_t)
        iw = jnp.exp(inter - m_t)
        qb = q.astype(BF16)
        kb = k.astype(BF16)
        s = lax.dot_general(qb, kb, _NT, preferred_element_type=F32) * dw
        c_old = c_s[h]
        n_old = n_s[h]
        num = (jnp.dot(s.astype(BF16), v.astype(BF16), preferred_element_type=F32)
               + iw * lax.dot_general(qb, c_old.astype(BF16), _NT, preferred_element_type=F32))
        den = jnp.sum(s, axis=1, keepdims=True) + iw * jnp.sum(q * n_old, axis=1, keepdims=True)
        hh = num / jnp.maximum(jnp.abs(den), jnp.exp(-m_t))
        m_new = m_t[CHUNK - 1:CHUNK, :]
        bt_last = bt_c[CHUNK - 1:CHUNK, :]
        wl_c = jnp.exp(bt_last - bt_c + it_c - m_new)
        decay = jnp.exp(bt_last + m_prev - m_new)
        vw = (v * wl_c).astype(BF16)
        c_s[h] = decay * c_old + lax.dot_general(vw, kb, _TN, preferred_element_type=F32)
        n_s[h] = decay * n_old + jnp.sum(wl_c * k, axis=0, keepdims=True)
        m_s[h] = jnp.broadcast_to(m_new, (1, LANES))
        hn = hh * lax.rsqrt(jnp.mean(hh * hh, axis=1, keepdims=True) + EPS) * gain_ref[:, vs]
        mp_ref[:, vs] = (_silu(z_ref[:, vs]) * _sigmoid(o_ref[:, vs]) * hn).astype(BF16)

    @pl.when(i == pl.num_programs(0) - 1)
    def _():
        c_out[...] = c_s[...]
        n_out[...] = n_s[...]
        m_out[...] = m_s[...]


def _mlstm_call(p2, gc, ct, gain):
    lp = p2.shape[0]
    nb = lp // CHUNK
    const3 = lambda i: (0, 0, 0)
    return pl.pallas_call(
        _mlstm_kernel,
        grid=(nb,),
        in_specs=[
            pl.BlockSpec((CHUNK, COLB), lambda i: (i, 1)),
            pl.BlockSpec((CHUNK, COLB), lambda i: (i, 2)),
            pl.BlockSpec((CHUNK, COLB), lambda i: (i, 3)),
            pl.BlockSpec((CHUNK, COLB), lambda i: (i, 4)),
            pl.BlockSpec((CHUNK, LANES), lambda i: (i, 0)),
            pl.BlockSpec((16, CHUNK), lambda i: (0, i)),
            pl.BlockSpec((1, M_WIDTH), lambda i: (0, 0)),
        ],
        out_specs=[
            pl.BlockSpec((CHUNK, M_WIDTH), lambda i: (i, 0)),
            pl.BlockSpec((M_HEADS, M_VDIM, M_QKDIM), const3),
            pl.BlockSpec((M_HEADS, 1, M_QKDIM), const3),
            pl.BlockSpec((M_HEADS, 1, LANES), const3),
        ],
        out_shape=[
            jax.ShapeDtypeStruct((lp, M_WIDTH), BF16),
            jax.ShapeDtypeStruct((M_HEADS, M_VDIM, M_QKDIM), F32),
            jax.ShapeDtypeStruct((M_HEADS, 1, M_QKDIM), F32),
            jax.ShapeDtypeStruct((M_HEADS, 1, LANES), F32),
        ],
        scratch_shapes=[
            pltpu.VMEM((M_HEADS, M_VDIM, M_QKDIM), F32),
            pltpu.VMEM((M_HEADS, 1, M_QKDIM), F32),
            pltpu.VMEM((M_HEADS, 1, LANES), F32),
        ],
        compiler_params=_cparams(("arbitrary",)),
        name="mlstm",
    )(p2, p2, p2, p2, gc, ct, gain)


def _out_kernel(x_ref, fm_ref, fn_ref, mm_ref, mn_ref, wt_ref, wb_ref, g_ref, y_ref):
    t = x_ref.shape[0]
    cat_f = jnp.concatenate([fm_ref[N_META:t, :], fn_ref[...]], axis=0)
    cat_m = jnp.concatenate([mm_ref[N_META:t, :], mn_ref[...]], axis=0)
    y = (x_ref[...]
         + jnp.dot(cat_f, wt_ref[...], preferred_element_type=F32)
         + jnp.dot(cat_m, wb_ref[...], preferred_element_type=F32))
    y_ref[...] = y * lax.rsqrt(jnp.mean(y * y, axis=-1, keepdims=True) + EPS) * g_ref[...]


def _out_call(x, fpart, mpart, w_top, w_bot, gain, tile):
    seq, d = x.shape
    per = tile // N_META
    main = lambda i: (i, 0)
    nxt = lambda i: ((i + 1) * per, 0)
    const = lambda i: (0, 0)
    return pl.pallas_call(
        _out_kernel,
        grid=(seq // tile,),
        in_specs=[
            pl.BlockSpec((tile, d), main),
            pl.BlockSpec((tile, F_WIDTH), main),
            pl.BlockSpec((N_META, F_WIDTH), nxt),
            pl.BlockSpec((tile, M_WIDTH), main),
            pl.BlockSpec((N_META, M_WIDTH), nxt),
            pl.BlockSpec((F_WIDTH, d), const),
            pl.BlockSpec((M_WIDTH, d), const),
            pl.BlockSpec((1, d), const),
        ],
        out_specs=pl.BlockSpec((tile, d), main),
        out_shape=jax.ShapeDtypeStruct((seq, d), F32),
        compiler_params=_cparams(("arbitrary",)),
        name="out_proj",
    )(x, fpart, fpart, mpart, mpart, w_top, w_bot, gain)


def _out_s_kernel(x_ref, f_ref, m_ref, wt_ref, wb_ref, g_ref, y_ref):
    y = (x_ref[...]
         + jnp.dot(f_ref[...].astype(BF16), wt_ref[...], preferred_element_type=F32)
         + jnp.dot(m_ref[...].astype(BF16), wb_ref[...], preferred_element_type=F32))
    y_ref[...] = y * lax.rsqrt(jnp.mean(y * y, axis=-1, keepdims=True) + EPS) * g_ref[...]


def _out_s_call(xs, fps, mps, w_top, w_bot, gain):
    s, d = xs.shape
    return pl.pallas_call(
        _out_s_kernel,
        out_shape=jax.ShapeDtypeStruct((s, d), F32),
        compiler_params=pltpu.CompilerParams(vmem_limit_bytes=VMEM_LIMIT),
        name="out_proj_s",
    )(xs, fps, mps, w_top, w_bot, gain)


def _bias_consts(page):
    w = page * F_HEADS
    src = np.arange(w)
    j, h = src // F_HEADS, src % F_HEADS
    dst = np.arange(F_HEADS * page)
    h2, j2 = dst // page, dst % page
    m_in = ((h[:, None] == h2[None, :]) & (j[:, None] > j2[None, :])).astype(np.float32)
    t_m = (h[:, None] == np.arange(LANES)[None, :]).astype(np.float32)
    e_m = (np.arange(LANES)[:, None] == h2[None, :]).astype(np.float32)
    return (jnp.asarray(m_in, BF16), jnp.asarray(t_m, BF16), jnp.asarray(e_m, BF16))


def _bias_kernel(pt_ref, lf_hbm, gs_ref, b_ref, min_ref, tm_ref, u_ref, e_ref, out_ref, x_ref, sem,
                 *, sb, n_pages):
    i = pl.program_id(0)
    rows = sb * n_pages

    def row_copy(r, page):
        return pltpu.make_async_copy(lf_hbm.at[pl.ds(page, 1)], x_ref.at[pl.ds(r, 1)], sem)

    def issue(r, c):
        row_copy(r, pt_ref[i * sb + r // n_pages, r % n_pages]).start()
        return c

    def drain(r, c):
        row_copy(r, 0).wait()
        return c

    lax.fori_loop(0, rows, issue, 0)
    lax.fori_loop(0, rows, drain, 0)

    x = x_ref[...]
    within = _dot3_r(x, min_ref[...])
    tot = _dot3_r(x, tm_ref[...])
    later = [_dot3_l(u_ref[...], tot[s * n_pages:(s + 1) * n_pages, :]) for s in range(sb)]
    later = _dot3_r(jnp.concatenate(later, axis=0), e_ref[...])
    cn = _dot3_r(_log_sigmoid(gs_ref[...] + b_ref[...]), e_ref[...])
    for s in range(sb):
        rs = slice(s * n_pages, (s + 1) * n_pages)
        out_ref[rs, :] = within[rs, :] + later[rs, :] + cn[s:s + 1, :]


def _bias_call(page_table, logf_rows, gs, bias_row, sb):
    b, n_pages = page_table.shape
    w = logf_rows.shape[1]
    page = w // F_HEADS
    m_in, t_m, e_m = _bias_consts(page)
    u = jnp.asarray(np.triu(np.ones((n_pages, n_pages), np.float32), 1), BF16)
    const = lambda i, pt: (0, 0)
    grid_spec = pltpu.PrefetchScalarGridSpec(
        num_scalar_prefetch=1,
        grid=(b // sb,),
        in_specs=[
            pl.BlockSpec(memory_space=pl.ANY),
            pl.BlockSpec((sb, LANES), lambda i, pt: (i, 0)),
            pl.BlockSpec((1, LANES), const),
            pl.BlockSpec((w, w), const),
            pl.BlockSpec((w, LANES), const),
            pl.BlockSpec((n_pages, n_pages), const),
            pl.BlockSpec((LANES, w), const),
        ],
        out_specs=pl.BlockSpec((sb * n_pages, w), lambda i, pt: (i, 0)),
        scratch_shapes=[pltpu.VMEM((sb * n_pages, w), F32), pltpu.SemaphoreType.DMA(())],
    )
    return pl.pallas_call(
        functools.partial(_bias_kernel, sb=sb, n_pages=n_pages),
        grid_spec=grid_spec,
        out_shape=jax.ShapeDtypeStruct((b * n_pages, w), F32),
        compiler_params=_cparams(("arbitrary",)),
        name="paged_bias",
    )(page_table, logf_rows, gs, bias_row, m_in, t_m, u, e_m)


def _paged_kernel(pt_ref, q_ref, kn_ref, vn_ref, *rest, group, scale):
    k_refs = rest[:group]
    v_refs = rest[group:2 * group]
    bias_ref, o_ref, qbd_ref, m_ref, l_ref, acc_ref = rest[2 * group:]
    g = pl.program_id(1)
    head = lax.broadcasted_iota(jnp.int32, (F_HEADS, F_WIDTH), 0)
    lane_head = lax.broadcasted_iota(jnp.int32, (F_HEADS, F_WIDTH), 1) // F_HDIM
    diag = head == lane_head

    @pl.when(g == 0)
    def _():
        qd = jnp.where(diag, q_ref[0], 0.0)
        qbd_ref[...] = qd.astype(BF16)
        m_ref[...] = jnp.sum(qd * kn_ref[0], axis=1, keepdims=True) * scale
        l_ref[...] = jnp.ones_like(l_ref)
        acc_ref[...] = jnp.broadcast_to(vn_ref[0], acc_ref.shape)

    qbd = qbd_ref[...]
    s = [lax.dot_general(qbd, k_refs[j][0].astype(BF16), _NT, preferred_element_type=F32) * scale
         + bias_ref[0, j] for j in range(group)]
    s = jnp.concatenate(s, axis=1)
    m_prev = m_ref[...]
    m_new = jnp.maximum(m_prev, jnp.max(s, axis=1, keepdims=True))
    alpha = jnp.exp(m_prev - m_new)
    p = jnp.exp(s - m_new)
    l_ref[...] = alpha * l_ref[...] + jnp.sum(p, axis=1, keepdims=True)
    pv = None
    for j in range(group):
        pj = p[:, j * F_HDIM:(j + 1) * F_HDIM].astype(BF16)
        d = jnp.dot(pj, v_refs[j][0].astype(BF16), preferred_element_type=F32)
        pv = d if pv is None else pv + d
    acc_ref[...] = alpha * acc_ref[...] + pv
    m_ref[...] = m_new

    @pl.when(g == pl.num_programs(1) - 1)
    def _():
        o = jnp.where(diag, acc_ref[...] / l_ref[...], 0.0)
        o_ref[0] = jnp.sum(o, axis=0, keepdims=True)


def _paged_call(page_table, ps3, k_pages, v_pages, bias4, group):
    b, n_pages = page_table.shape
    page = k_pages.shape[1]

    def page_map(j):
        return lambda bi, g, pt: (pt[bi, g * group + j], 0, 0)

    row = lambda c: pl.BlockSpec((1, 1, COLB), lambda bi, g, pt: (bi, 0, c))
    grid_spec = pltpu.PrefetchScalarGridSpec(
        num_scalar_prefetch=1,
        grid=(b, n_pages // group),
        in_specs=(
            [row(0), row(1), row(2)]
            + [pl.BlockSpec((1, page, F_WIDTH), page_map(j)) for j in range(group)]
            + [pl.BlockSpec((1, page, F_WIDTH), page_map(j)) for j in range(group)]
            + [pl.BlockSpec((1, group, F_HEADS, page), lambda bi, g, pt: (bi, g, 0, 0))]
        ),
        out_specs=pl.BlockSpec((1, 1, F_WIDTH), lambda bi, g, pt: (bi, 0, 0)),
        scratch_shapes=[
            pltpu.VMEM((F_HEADS, F_WIDTH), BF16),
            pltpu.VMEM((F_HEADS, 1), F32),
            pltpu.VMEM((F_HEADS, 1), F32),
            pltpu.VMEM((F_HEADS, F_WIDTH), F32),
        ],
    )
    return pl.pallas_call(
        functools.partial(_paged_kernel, group=group, scale=F_HDIM ** -0.5),
        grid_spec=grid_spec,
        out_shape=jax.ShapeDtypeStruct((b, 1, F_WIDTH), F32),
        compiler_params=_cparams(("arbitrary", "arbitrary")),
        name="paged_attn",
    )(page_table, ps3, ps3, ps3, *([k_pages] * group), *([v_pages] * group), bias4)


def _step_kernel(qk_ref, v_ref, o_ref, z_ref, fz_ref, gs_ref, b_ref, fo_ref, gain_ref,
                 c_ref, n_ref, m_ref,
                 c_out, n_out, m_out, lf_out, fp_out, mp_out, *, sb):
    g = gs_ref[...] + b_ref[...]
    ls = _log_sigmoid(g)
    lf_out[...] = ls[:, G_FOX:G_FOX + F_HEADS]
    fp_out[...] = _silu(fz_ref[...]) * fo_ref[...]
    first_row = lax.broadcasted_iota(jnp.int32, (CHUNK, 1), 0) == 0
    for b in range(sb):
        for h in range(M_HEADS):
            qs = slice(h * M_QKDIM, (h + 1) * M_QKDIM)
            ks = slice(M_QKW + h * M_QKDIM, M_QKW + (h + 1) * M_QKDIM)
            vs = slice(h * M_VDIM, (h + 1) * M_VDIM)
            q = qk_ref[b:b + 1, qs]
            k = qk_ref[b:b + 1, ks] * (M_QKDIM ** -0.5)
            v = v_ref[b:b + 1, vs]
            logi = g[b:b + 1, G_MI + h:G_MI + h + 1]
            logf = ls[b:b + 1, G_MF + h:G_MF + h + 1]
            m_prev = m_ref[b:b + 1, h:h + 1]
            inter = logf + m_prev
            m_t = jnp.maximum(inter, logi)
            dw = jnp.exp(logi - m_t)
            iw = jnp.exp(inter - m_t)
            c_old = c_ref[b, h]
            n_old = n_ref[b, h:h + 1, :]
            s = jnp.sum(q * k, axis=1, keepdims=True) * dw
            cq = lax.dot_general(q.astype(BF16), c_old.astype(BF16), _NT, preferred_element_type=F32)
            num = s * v + iw * cq
            den = s + iw * jnp.sum(q * n_old, axis=1, keepdims=True)
            hh = num / jnp.maximum(jnp.abs(den), jnp.exp(-m_t))
            v_pad = jnp.where(first_row, jnp.broadcast_to(v * dw, (CHUNK, M_VDIM)), 0.0).astype(BF16)
            k_pad = jnp.broadcast_to(k, (CHUNK, M_QKDIM)).astype(BF16)
            c_out[b, h] = iw * c_old + lax.dot_general(v_pad, k_pad, _TN, preferred_element_type=F32)
            n_out[b, h:h + 1, :] = iw * n_old + dw * k
            m_out[b:b + 1, h:h + 1] = m_t
            hn = hh * lax.rsqrt(jnp.mean(hh * hh, axis=1, keepdims=True) + EPS) * gain_ref[:, vs]
            mp_out[b:b + 1, vs] = _silu(z_ref[b:b + 1, vs]) * _sigmoid(o_ref[b:b + 1, vs]) * hn


def _step_call(ps, gs, bias_row, fo, gain, state_c, state_n, state_m, sb):
    b = ps.shape[0]
    col = lambda c: pl.BlockSpec((sb, COLB), lambda i: (i, c))
    const = lambda i: (0, 0)
    return pl.pallas_call(
        functools.partial(_step_kernel, sb=sb),
        grid=(b // sb,),
        in_specs=[
            col(4), col(5), col(6), col(7), col(3),
            pl.BlockSpec((sb, LANES), lambda i: (i, 0)),
            pl.BlockSpec((1, LANES), const),
            pl.BlockSpec((sb, F_WIDTH), lambda i: (i, 0)),
            pl.BlockSpec((1, M_WIDTH), const),
            pl.BlockSpec((sb, M_HEADS, M_VDIM, M_QKDIM), lambda i: (i, 0, 0, 0)),
            pl.BlockSpec((sb, M_HEADS, M_QKDIM), lambda i: (i, 0, 0)),
            pl.BlockSpec((sb, M_HEADS), lambda i: (i, 0)),
        ],
        out_specs=[
            pl.BlockSpec((sb, M_HEADS, M_VDIM, M_QKDIM), lambda i: (i, 0, 0, 0)),
            pl.BlockSpec((sb, M_HEADS, M_QKDIM), lambda i: (i, 0, 0)),
            pl.BlockSpec((sb, M_HEADS), lambda i: (i, 0)),
            pl.BlockSpec((sb, F_HEADS), lambda i: (i, 0)),
            pl.BlockSpec((sb, F_WIDTH), lambda i: (i, 0)),
            pl.BlockSpec((sb, M_WIDTH), lambda i: (i, 0)),
        ],
        out_shape=[
            jax.ShapeDtypeStruct(state_c.shape, F32),
            jax.ShapeDtypeStruct(state_n.shape, F32),
            jax.ShapeDtypeStruct(state_m.shape, F32),
            jax.ShapeDtypeStruct((b, F_HEADS), F32),
            jax.ShapeDtypeStruct((b, F_WIDTH), F32),
            jax.ShapeDtypeStruct((b, M_WIDTH), F32),
        ],
        compiler_params=_cparams(("arbitrary",)),
        name="mlstm_step",
    )(ps, ps, ps, ps, ps, gs, bias_row, fo, gain, state_c, state_n, state_m)


def _pack_w_in(w):
    o = np.concatenate([[0], np.cumsum(PROJ_SIZES)])
    fq, fk, fv, ff, fz, mq, mk, mv, mi, mf, mo, mz = [w[:, o[i]:o[i + 1]] for i in range(12)]
    main = jnp.concatenate([fq, fk, fv, fz, mq, mk, mv, mo, mz], axis=1).astype(BF16)
    pad = jnp.zeros((w.shape[0], LANES - F_HEADS - 2 * M_HEADS), w.dtype)
    gate = jnp.concatenate([ff, mf, mi, pad], axis=1).astype(BF16)
    return main, gate


def kernel(x_prompt, x_sample, cache_k, cache_v, cache_logf, state_C, state_n, state_m,
           page_table, meta_tokens, norm_gain, w_in, b_fox_f, b_m_i, b_m_f, mh_gain,
           w_out, final_gain):
    batch, seq, d = x_prompt.shape
    depth = w_in.shape[0]
    assert batch == 1 and depth == 1 and x_sample.shape[1] == 1
    assert seq % CHUNK == 0
    s = x_sample.shape[0]
    n_real = N_META + seq
    lp = pl.cdiv(n_real, CHUNK) * CHUNK
    tile = _pick_tile(lp, (640, 512, 384, 256, 128))
    tile_out = _pick_tile(seq, (512, 256, 128))
    n_pool, page = cache_k.shape[1], cache_k.shape[2]
    n_pages = page_table.shape[1]
    assert page == CHUNK and n_pages % 4 == 0 and s % 8 == 0

    w_main, w_gate = _pack_w_in(w_in[0])
    w_o = w_out[0].astype(BF16)
    w_top, w_bot = w_o[:F_WIDTH], w_o[F_WIDTH:]
    bias_row = jnp.concatenate(
        [b_fox_f[0], b_m_f[0], b_m_i[0], jnp.zeros((LANES - F_HEADS - 2 * M_HEADS,), F32)])[None, :]
    gain = norm_gain[0][None, :]
    mh = mh_gain[0][None, :]
    fgain = final_gain[None, :]
    x2 = x_prompt[0]
    xs2 = x_sample[:, 0, :]

    xn, xsn = _norm_call(x2, meta_tokens, gain, xs2, lp, tile)
    q_bf, k_out, k_bf, v_out, v_bf, p2, gates, ps, gs = _inproj_call(xn, w_main, w_gate, xsn, n_real, tile)
    gc, ct, lf_p = _gates_call(gates, bias_row, n_real)

    fpart = _fox_call(q_bf, k_bf, v_bf, ct, gc, p2, tile)
    mpart, c_p, n_p, m_p = _mlstm_call(p2, gc, ct, mh)
    y_p = _out_call(x2, fpart, mpart, w_top, w_bot, fgain, tile_out)

    lf_rows = cache_logf.reshape(depth * n_pool, page * F_HEADS)
    bias = _bias_call(page_table, lf_rows, gs, bias_row, sb=8)
    fo = _paged_call(page_table, ps.reshape(s, 1, N_COLB * COLB),
                     cache_k.reshape(depth * n_pool, page, F_WIDTH),
                     cache_v.reshape(depth * n_pool, page, F_WIDTH),
                     bias.reshape(s, n_pages, F_HEADS, page), group=4)
    c_s, n_s, m_s, lf_s, fps, mps = _step_call(ps, gs, bias_row, fo.reshape(s, F_WIDTH), mh,
                                               state_C[0], state_n[0], state_m[0], sb=8)
    y_s = _out_s_call(xs2, fps, mps, w_top, w_bot, fgain)

    hd = (F_HEADS, F_HDIM)
    return (
        y_p[None],
        y_s[:, None, :],
        k_out.reshape(1, 1, n_real, *hd),
        v_out.reshape(1, 1, n_real, *hd),
        lf_p.reshape(1, 1, n_real, F_HEADS),
        c_p[None, None],
        n_p[:, 0, :][None, None],
        m_p[:, 0, 0][None, None],
        ps[:, COLB:2 * COLB].reshape(1, s, 1, *hd),
        ps[:, 2 * COLB:3 * COLB].reshape(1, s, 1, *hd),
        lf_s.reshape(1, s, 1, F_HEADS),
        c_s[None],
        n_s[None],
        m_s[None],
    )
```

```python
import functools

import numpy as np
import jax
import jax.numpy as jnp
from jax import lax
from jax.experimental import pallas as pl
from jax.experimental.pallas import tpu as pltpu

F32 = jnp.float32
BF16 = jnp.bfloat16

N_META = 16
EPS = 1e-6
F_HEADS = 8
F_HDIM = 128
F_WIDTH = F_HEADS * F_HDIM
M_HEADS = 4
M_VDIM = 256
M_QKDIM = 128
M_WIDTH = M_HEADS * M_VDIM
M_QKW = M_HEADS * M_QKDIM
PROJ_SIZES = (F_WIDTH, F_WIDTH, F_WIDTH, F_HEADS, F_WIDTH,
              M_QKW, M_QKW, M_WIDTH, M_HEADS, M_HEADS, M_WIDTH, M_WIDTH)

LANES = 128
CHUNK = 128
COLB = 1024
N_COLB = 8
G_FOX = 0
G_MF = 8
G_MI = 12
NEG = -1e30
LOG2E = 1.4426950408889634
VMEM_LIMIT = 56 * 1024 * 1024

_NT = (((1,), (1,)), ((), ()))
_TN = (((0,), (0,)), ((), ()))


def _cparams(sem, vmem=VMEM_LIMIT):
    return pltpu.CompilerParams(dimension_semantics=sem, vmem_limit_bytes=vmem)


def _pick_tile(n, cands):
    for c in cands:
        if n % c == 0:
            return c
    raise ValueError(f"no tile for {n}")


def _log_sigmoid(x):
    return jnp.minimum(x, 0.0) - jnp.log1p(jnp.exp(-jnp.abs(x)))


def _sigmoid(x):
    return 1.0 / (1.0 + jnp.exp(-x))


def _silu(x):
    return x * _sigmoid(x)


def _split3(x):
    x1 = x.astype(BF16)
    r1 = x - x1.astype(F32)
    x2 = r1.astype(BF16)
    r2 = r1 - x2.astype(F32)
    x3 = r2.astype(BF16)
    return x1, x2, x3


def _dot3_l(a_bf, x):
    x1, x2, x3 = _split3(x)
    d = lambda v: jnp.dot(a_bf, v, preferred_element_type=F32)
    return (d(x3) + d(x2)) + d(x1)


def _dot3_r(x, b_bf):
    x1, x2, x3 = _split3(x)
    d = lambda v: jnp.dot(v, b_bf, preferred_element_type=F32)
    return (d(x3) + d(x2)) + d(x1)


def _norm_kernel(x_ref, prev_ref, meta_ref, g_ref, xs_ref, xn_ref, xsn_ref, *, tile, n_real):
    i = pl.program_id(0)
    g = g_ref[...]

    def nrm(x):
        return x * lax.rsqrt(jnp.mean(x * x, axis=-1, keepdims=True) + EPS) * g

    top = jnp.where(i == 0, meta_ref[...], prev_ref[...])
    xn_ref[0:N_META, :] = nrm(top).astype(BF16)
    body = x_ref[0:tile - N_META, :]
    row = i * tile + N_META + lax.broadcasted_iota(jnp.int32, (tile - N_META, 1), 0)
    xn_ref[N_META:tile, :] = jnp.where(row < n_real, nrm(body), 0.0).astype(BF16)

    @pl.when(i == 0)
    def _():
        xsn_ref[...] = nrm(xs_ref[...]).astype(BF16)


def _norm_call(x, meta, gain, xs, lp, tile):
    seq, d = x.shape
    n_real = seq + N_META
    nblk = pl.cdiv(seq, tile)
    per = tile // N_META
    nprev = seq // N_META
    s = xs.shape[0]
    return pl.pallas_call(
        functools.partial(_norm_kernel, tile=tile, n_real=n_real),
        grid=(lp // tile,),
        in_specs=[
            pl.BlockSpec((tile, d), lambda i: (jnp.minimum(i, nblk - 1), 0)),
            pl.BlockSpec((N_META, d), lambda i: (jnp.clip(i * per - 1, 0, nprev - 1), 0)),
            pl.BlockSpec((N_META, d), lambda i: (0, 0)),
            pl.BlockSpec((1, d), lambda i: (0, 0)),
            pl.BlockSpec((s, d), lambda i: (0, 0)),
        ],
        out_specs=[
            pl.BlockSpec((tile, d), lambda i: (i, 0)),
            pl.BlockSpec((s, d), lambda i: (0, 0)),
        ],
        out_shape=[jax.ShapeDtypeStruct((lp, d), BF16), jax.ShapeDtypeStruct((s, d), BF16)],
        compiler_params=_cparams(("arbitrary",)),
        name="norm",
    )(x, x, meta, gain, xs)


def _inproj_kernel(x_ref, w_ref, wg_ref, xs_ref,
                   q_ref, ko_ref, kb_ref, vo_ref, vb_ref, p2_ref, g_ref, ps_ref, gs_ref):
    n = pl.program_id(0)
    m = pl.program_id(1)
    acc = jnp.dot(x_ref[...], w_ref[...], preferred_element_type=F32)

    @pl.when(n == 0)
    def _():
        q_ref[...] = acc.T.astype(BF16)
        g_ref[...] = jnp.dot(x_ref[...], wg_ref[...], preferred_element_type=F32)

    @pl.when(n == 1)
    def _():
        ko_ref[...] = acc
        kb_ref[...] = acc.astype(BF16)

    @pl.when(n == 2)
    def _():
        vo_ref[...] = acc
        vb_ref[...] = acc.T.astype(BF16)

    @pl.when(n >= 3)
    def _():
        p2_ref[...] = acc

    @pl.when(m == 0)
    def _():
        ps_ref[...] = jnp.dot(xs_ref[...], w_ref[...], preferred_element_type=F32)

    @pl.when((m == 0) & (n == 0))
    def _():
        gs_ref[...] = jnp.dot(xs_ref[...], wg_ref[...], preferred_element_type=F32)


def _inproj_call(xn, w_main, w_gate, xsn, n_real, tile):
    lp, d = xn.shape
    s = xsn.shape[0]
    nm = lp // tile

    def held(n, m, lo, hi):
        return jnp.where(n < lo, 0, jnp.where(n > hi, nm - 1, m))

    def own(lo, hi):
        return lambda n, m: (held(n, m, lo, hi), 0)

    def own_t(lo, hi):
        return lambda n, m: (0, held(n, m, lo, hi))

    return pl.pallas_call(
        _inproj_kernel,
        grid=(N_COLB, nm),
        in_specs=[
            pl.BlockSpec((tile, d), lambda n, m: (m, 0)),
            pl.BlockSpec((d, COLB), lambda n, m: (0, n)),
            pl.BlockSpec((d, LANES), lambda n, m: (0, 0)),
            pl.BlockSpec((s, d), lambda n, m: (0, 0)),
        ],
        out_specs=[
            pl.BlockSpec((COLB, tile), own_t(0, 0)),
            pl.BlockSpec((tile, COLB), own(1, 1)),
            pl.BlockSpec((tile, COLB), own(1, 1)),
            pl.BlockSpec((tile, COLB), own(2, 2)),
            pl.BlockSpec((COLB, tile), own_t(2, 2)),
            pl.BlockSpec((tile, COLB), lambda n, m: (jnp.where(n < 3, 0, m), jnp.maximum(n - 3, 0))),
            pl.BlockSpec((tile, LANES), own(0, 0)),
            pl.BlockSpec((s, COLB), lambda n, m: (0, n)),
            pl.BlockSpec((s, LANES), lambda n, m: (0, 0)),
        ],
        out_shape=[
            jax.ShapeDtypeStruct((COLB, lp), BF16),
            jax.ShapeDtypeStruct((n_real, COLB), F32),
            jax.ShapeDtypeStruct((lp, COLB), BF16),
            jax.ShapeDtypeStruct((n_real, COLB), F32),
            jax.ShapeDtypeStruct((COLB, lp), BF16),
            jax.ShapeDtypeStruct((lp, (N_COLB - 3) * COLB), F32),
            jax.ShapeDtypeStruct((lp, LANES), F32),
            jax.ShapeDtypeStruct((s, N_COLB * COLB), F32),
            jax.ShapeDtypeStruct((s, LANES), F32),
        ],
        compiler_params=_cparams(("arbitrary", "arbitrary")),
        name="in_proj",
    )(xn, w_main, w_gate, xsn)


def _gates_kernel(g_ref, b_ref, ltri_ref, gc_ref, ct_ref, lf_ref, carry_ref, *, n_real):
    i = pl.program_id(0)

    @pl.when(i == 0)
    def _():
        carry_ref[...] = jnp.zeros_like(carry_ref)

    x = g_ref[...] + b_ref[...]
    lane = lax.broadcasted_iota(jnp.int32, (CHUNK, LANES), 1)
    row = i * CHUNK + lax.broadcasted_iota(jnp.int32, (CHUNK, LANES), 0)
    valid = row < n_real
    ls = _log_sigmoid(x)
    summed = jnp.where(valid & (lane < G_MI), ls, 0.0)
    cs = _dot3_l(ltri_ref[...], summed)
    glob = cs + carry_ref[...]
    out = jnp.where(lane < G_MF, glob,
                    jnp.where(lane < G_MI, cs, jnp.where(valid, x, NEG)))
    gc_ref[...] = out
    ct_ref[...] = out.T[0:16, :]
    lf_ref[...] = ls[:, 0:F_HEADS]
    carry_ref[...] = glob[CHUNK - 1:CHUNK, :]


def _gates_call(gates, bias_row, n_real):
    lp = gates.shape[0]
    nb = lp // CHUNK
    ltri = jnp.asarray(np.tril(np.ones((CHUNK, CHUNK), np.float32)), BF16)
    return pl.pallas_call(
        functools.partial(_gates_kernel, n_real=n_real),
        grid=(nb,),
        in_specs=[
            pl.BlockSpec((CHUNK, LANES), lambda i: (i, 0)),
            pl.BlockSpec((1, LANES), lambda i: (0, 0)),
            pl.BlockSpec((CHUNK, CHUNK), lambda i: (0, 0)),
        ],
        out_specs=[
            pl.BlockSpec((CHUNK, LANES), lambda i: (i, 0)),
            pl.BlockSpec((16, CHUNK), lambda i: (0, i)),
            pl.BlockSpec((CHUNK, F_HEADS), lambda i: (i, 0)),
        ],
        out_shape=[
            jax.ShapeDtypeStruct((lp, LANES), F32),
            jax.ShapeDtypeStruct((16, lp), F32),
            jax.ShapeDtypeStruct((n_real, F_HEADS), F32),
        ],
        scratch_shapes=[pltpu.VMEM((1, LANES), F32)],
        compiler_params=_cparams(("arbitrary",)),
        name="gates",
    )(gates, bias_row, ltri)


def _fox_kernel(qi_ref, ki_ref, qt_ref, k_ref, vt_ref, gck_ref, ctq_ref, fz_ref, o_ref,
                acc_ref, m_ref, l_ref, *, scale):
    step_id = pl.program_id(0)
    qi = qi_ref[step_id]
    ki = ki_ref[step_id]
    tk = k_ref.shape[0]
    ncol = qt_ref.shape[1] // LANES
    c1 = scale * LOG2E

    @pl.when(ki == 0)
    def _():
        m_ref[...] = jnp.full_like(m_ref, -jnp.inf)
        l_ref[...] = jnp.zeros_like(l_ref)
        acc_ref[...] = jnp.zeros_like(acc_ref)

    def step(diag):
        for h in range(F_HEADS):
            hs = slice(h * F_HDIM, (h + 1) * F_HDIM)
            ckb = jnp.broadcast_to(gck_ref[:, G_FOX + h:G_FOX + h + 1] * LOG2E, (tk, LANES))
            for c in range(ncol):
                cs = slice(c * LANES, (c + 1) * LANES)
                nk = (c + 1) * LANES if diag else tk
                s = jnp.dot(k_ref[0:nk, hs], qt_ref[hs, cs], preferred_element_type=F32)
                t = s * c1 - ckb[0:nk, :]
                if diag:
                    keep = (lax.broadcasted_iota(jnp.int32, (nk, LANES), 0)
                            <= c * LANES + lax.broadcasted_iota(jnp.int32, (nk, LANES), 1))
                    t = jnp.where(keep, t, -jnp.inf)
                cq = ctq_ref[G_FOX + h:G_FOX + h + 1, cs] * LOG2E
                m_prev = m_ref[h, :, cs]
                m_new = jnp.maximum(m_prev, jnp.max(t, axis=0, keepdims=True) + cq)
                p = jnp.exp2(t - (m_new - cq))
                alpha = jnp.exp2(m_prev - m_new)
                l_ref[h, :, cs] = alpha * l_ref[h, :, cs] + jnp.sum(p, axis=0, keepdims=True)
                acc_ref[hs, cs] = alpha * acc_ref[hs, cs] + jnp.dot(
                    vt_ref[hs, 0:nk], p.astype(BF16), preferred_element_type=F32)
                m_ref[h, :, cs] = m_new

    @pl.when(ki < qi)
    def _():
        step(False)

    @pl.when(ki == qi)
    def _():
        step(True)
        for h in range(F_HEADS):
            hs = slice(h * F_HDIM, (h + 1) * F_HDIM)
            for c in range(ncol):
                cs = slice(c * LANES, (c + 1) * LANES)
                o = (acc_ref[hs, cs] / l_ref[h, :, cs]).T
                o_ref[cs, hs] = (_silu(fz_ref[cs, hs]) * o).astype(BF16)


def _fox_call(qt, k, vt, ct, gc, p2, tile):
    lp = k.shape[0]
    nq = lp // tile
    pairs = [(q, kk) for q in range(nq) for kk in range(q + 1)]
    qi_list = jnp.asarray(np.array([p[0] for p in pairs], np.int32))
    ki_list = jnp.asarray(np.array([p[1] for p in pairs], np.int32))
    grid_spec = pltpu.PrefetchScalarGridSpec(
        num_scalar_prefetch=2,
        grid=(len(pairs),),
        in_specs=[
            pl.BlockSpec((F_WIDTH, tile), lambda s, qi, ki: (0, qi[s])),
            pl.BlockSpec((tile, F_WIDTH), lambda s, qi, ki: (ki[s], 0)),
            pl.BlockSpec((F_WIDTH, tile), lambda s, qi, ki: (0, ki[s])),
            pl.BlockSpec((tile, LANES), lambda s, qi, ki: (ki[s], 0)),
            pl.BlockSpec((16, tile), lambda s, qi, ki: (0, qi[s])),
            pl.BlockSpec((tile, COLB), lambda s, qi, ki: (qi[s], 0)),
        ],
        out_specs=pl.BlockSpec((tile, F_WIDTH), lambda s, qi, ki: (qi[s], 0)),
        scratch_shapes=[
            pltpu.VMEM((F_WIDTH, tile), F32),
            pltpu.VMEM((F_HEADS, 1, tile), F32),
            pltpu.VMEM((F_HEADS, 1, tile), F32),
        ],
    )
    return pl.pallas_call(
        functools.partial(_fox_kernel, scale=F_HDIM ** -0.5),
        grid_spec=grid_spec,
        out_shape=jax.ShapeDtypeStruct((lp, F_WIDTH), BF16),
        compiler_params=_cparams(("arbitrary",)),
        name="fox",
    )(qi_list, ki_list, qt, k, vt, gc, ct, p2)


def _mlstm_kernel(qk_ref, v_ref, o_ref, z_ref, gc_ref, ct_ref, gain_ref,
                  mp_ref, c_out, n_out, m_out, c_s, n_s, m_s):
    i = pl.program_id(0)

    @pl.when(i == 0)
    def _():
        c_s[...] = jnp.zeros_like(c_s)
        n_s[...] = jnp.zeros_like(n_s)
        m_s[...] = jnp.zeros_like(m_s)

    causal = (lax.broadcasted_iota(jnp.int32, (CHUNK, CHUNK), 0)
              >= lax.broadcasted_iota(jnp.int32, (CHUNK, CHUNK), 1))
    for h in range(M_HEADS):
        qs = slice(h * M_QKDIM, (h + 1) * M_QKDIM)
        ks = slice(M_QKW + h * M_QKDIM, M_QKW + (h + 1) * M_QKDIM)
        vs = slice(h * M_VDIM, (h + 1) * M_VDIM)
        q = qk_ref[:, qs]
        k = qk_ref[:, ks] * (M_QKDIM ** -0.5)
        v = v_ref[:, vs]
        bt_c = gc_ref[:, G_MF + h:G_MF + h + 1]
        it_c = gc_ref[:, G_MI + h:G_MI + h + 1]
        bt_r = ct_ref[G_MF + h:G_MF + h + 1, :]
        it_r = ct_ref[G_MI + h:G_MI + h + 1, :]
        m_prev = m_s[h][:, 0:1]
        dlog = jnp.where(causal, bt_c - bt_r + it_r, -jnp.inf)
        inter = bt_c + m_prev
        m_t = jnp.maximum(inter, jnp.max(dlog, axis=1, keepdims=True))
        dw = jnp.exp(dlog - m_t)
        iw = jnp.exp(inter - m_t)
        qb = q.astype(BF16)
        kb = k.astype(BF16)
        s = lax.dot_general(qb, kb, _NT, preferred_element_type=F32) * dw
        c_old = c_s[h]
        n_old = n_s[h]
        num = (jnp.dot(s.astype(BF16), v.astype(BF16), preferred_element_type=F32)
               + iw * lax.dot_general(qb, c_old.astype(BF16), _NT, preferred_element_type=F32))
        den = jnp.sum(s, axis=1, keepdims=True) + iw * jnp.sum(q * n_old, axis=1, keepdims=True)
        hh = num / jnp.maximum(jnp.abs(den), jnp.exp(-m_t))
        m_new = m_t[CHUNK - 1:CHUNK, :]
        bt_last = bt_c[CHUNK - 1:CHUNK, :]
        wl_c = jnp.exp(bt_last - bt_c + it_c - m_new)
        decay = jnp.exp(bt_last + m_prev - m_new)
        vw = (v * wl_c).astype(BF16)
        c_s[h] = decay * c_old + lax.dot_general(vw, kb, _TN, preferred_element_type=F32)
        n_s[h] = decay * n_old + jnp.sum(wl_c * k, axis=0, keepdims=True)
        m_s[h] = jnp.broadcast_to(m_new, (1, LANES))
        hn = hh * lax.rsqrt(jnp.mean(hh * hh, axis=1, keepdims=True) + EPS) * gain_ref[:, vs]
        mp_ref[:, vs] = (_silu(z_ref[:, vs]) * _sigmoid(o_ref[:, vs]) * hn).astype(BF16)

    @pl.when(i == pl.num_programs(0) - 1)
    def _():
        c_out[...] = c_s[...]
        n_out[...] = n_s[...]
        m_out[...] = m_s[...]


def _mlstm_call(p2, gc, ct, gain):
    lp = p2.shape[0]
    nb = lp // CHUNK
    const3 = lambda i: (0, 0, 0)
    return pl.pallas_call(
        _mlstm_kernel,
        grid=(nb,),
        in_specs=[
            pl.BlockSpec((CHUNK, COLB), lambda i: (i, 1)),
            pl.BlockSpec((CHUNK, COLB), lambda i: (i, 2)),
            pl.BlockSpec((CHUNK, COLB), lambda i: (i, 3)),
            pl.BlockSpec((CHUNK, COLB), lambda i: (i, 4)),
            pl.BlockSpec((CHUNK, LANES), lambda i: (i, 0)),
            pl.BlockSpec((16, CHUNK), lambda i: (0, i)),
            pl.BlockSpec((1, M_WIDTH), lambda i: (0, 0)),
        ],
        out_specs=[
            pl.BlockSpec((CHUNK, M_WIDTH), lambda i: (i, 0)),
            pl.BlockSpec((M_HEADS, M_VDIM, M_QKDIM), const3),
            pl.BlockSpec((M_HEADS, 1, M_QKDIM), const3),
            pl.BlockSpec((M_HEADS, 1, LANES), const3),
        ],
        out_shape=[
            jax.ShapeDtypeStruct((lp, M_WIDTH), BF16),
            jax.ShapeDtypeStruct((M_HEADS, M_VDIM, M_QKDIM), F32),
            jax.ShapeDtypeStruct((M_HEADS, 1, M_QKDIM), F32),
            jax.ShapeDtypeStruct((M_HEADS, 1, LANES), F32),
        ],
        scratch_shapes=[
            pltpu.VMEM((M_HEADS, M_VDIM, M_QKDIM), F32),
            pltpu.VMEM((M_HEADS, 1, M_QKDIM), F32),
            pltpu.VMEM((M_HEADS, 1, LANES), F32),
        ],
        compiler_params=_cparams(("arbitrary",)),
        name="mlstm",
    )(p2, p2, p2, p2, gc, ct, gain)


def _out_kernel(x_ref, fm_ref, fn_ref, mm_ref, mn_ref, wt_ref, wb_ref, g_ref, y_ref):
    t = x_ref.shape[0]
    cat_f = jnp.concatenate([fm_ref[N_META:t, :], fn_ref[...]], axis=0)
    cat_m = jnp.concatenate([mm_ref[N_META:t, :], mn_ref[...]], axis=0)
    y = (x_ref[...]
         + jnp.dot(cat_f, wt_ref[...], preferred_element_type=F32)
         + jnp.dot(cat_m, wb_ref[...], preferred_element_type=F32))
    y_ref[...] = y * lax.rsqrt(jnp.mean(y * y, axis=-1, keepdims=True) + EPS) * g_ref[...]


def _out_call(x, fpart, mpart, w_top, w_bot, gain, tile):
    seq, d = x.shape
    per = tile // N_META
    main = lambda i: (i, 0)
    nxt = lambda i: ((i + 1) * per, 0)
    const = lambda i: (0, 0)
    return pl.pallas_call(
        _out_kernel,
        grid=(seq // tile,),
        in_specs=[
            pl.BlockSpec((tile, d), main),
            pl.BlockSpec((tile, F_WIDTH), main),
            pl.BlockSpec((N_META, F_WIDTH), nxt),
            pl.BlockSpec((tile, M_WIDTH), main),
            pl.BlockSpec((N_META, M_WIDTH), nxt),
            pl.BlockSpec((F_WIDTH, d), const),
            pl.BlockSpec((M_WIDTH, d), const),
            pl.BlockSpec((1, d), const),
        ],
        out_specs=pl.BlockSpec((tile, d), main),
        out_shape=jax.ShapeDtypeStruct((seq, d), F32),
        compiler_params=_cparams(("arbitrary",)),
        name="out_proj",
    )(x, fpart, fpart, mpart, mpart, w_top, w_bot, gain)


def _out_s_kernel(x_ref, f_ref, m_ref, wt_ref, wb_ref, g_ref, y_ref):
    y = (x_ref[...]
         + jnp.dot(f_ref[...].astype(BF16), wt_ref[...], preferred_element_type=F32)
         + jnp.dot(m_ref[...].astype(BF16), wb_ref[...], preferred_element_type=F32))
    y_ref[...] = y * lax.rsqrt(jnp.mean(y * y, axis=-1, keepdims=True) + EPS) * g_ref[...]


def _out_s_call(xs, fps, mps, w_top, w_bot, gain):
    s, d = xs.shape
    return pl.pallas_call(
        _out_s_kernel,
        out_shape=jax.ShapeDtypeStruct((s, d), F32),
        compiler_params=pltpu.CompilerParams(vmem_limit_bytes=VMEM_LIMIT),
        name="out_proj_s",
    )(xs, fps, mps, w_top, w_bot, gain)


def _flat_consts(page, group):
    lane = np.arange(LANES)
    b, h = lane // F_HEADS, lane % F_HEADS
    same = h[:, None] == h[None, :]
    m_blk = (same & (b[:, None] > b[None, :])).astype(np.float32)
    t_blk = same.astype(np.float32)
    r = np.arange(group * (page * F_HEADS // LANES))
    k, p = r // group, r % group
    u_row = ((p[None, :] > p[:, None]) | ((p[None, :] == p[:, None]) & (k[None, :] > k[:, None])))
    return (jnp.asarray(m_blk, BF16), jnp.asarray(t_blk, BF16), jnp.asarray(u_row.astype(np.float32), BF16))


def _paged_kernel(pt_ref, q_ref, kn_ref, vn_ref, gs_ref, b_ref, mb_ref, tb_ref, u_ref, *rest,
                  group, scale):
    k_refs = rest[:group]
    v_refs = rest[group:2 * group]
    lf_refs = rest[2 * group:3 * group]
    o_ref, m_ref, l_ref, acc_ref, base_ref = rest[3 * group:]
    g = pl.program_id(1)
    w = lf_refs[0].shape[-1]
    nblk = w // LANES
    own = (lax.broadcasted_iota(jnp.int32, (F_HEADS, w), 1) % F_HEADS
           == lax.broadcasted_iota(jnp.int32, (F_HEADS, w), 0))

    @pl.when(g == 0)
    def _():
        m_ref[...] = jnp.sum(q_ref[0] * kn_ref[0], axis=1, keepdims=True) * scale
        l_ref[...] = jnp.ones_like(l_ref)
        acc_ref[...] = vn_ref[0]
        cn = _log_sigmoid(gs_ref[0] + b_ref[...])
        head_lane = lax.broadcasted_iota(jnp.int32, (1, LANES), 1) < F_HEADS
        base_ref[...] = _dot3_r(jnp.where(head_lane, cn, 0.0), tb_ref[...])

    x = jnp.concatenate([r[0] for r in lf_refs], axis=0)
    xs = jnp.concatenate([x[:, k * LANES:(k + 1) * LANES] for k in range(nblk)], axis=0)
    tot = _dot3_r(xs, tb_ref[...])
    bias = _dot3_r(xs, mb_ref[...]) + _dot3_l(u_ref[...], tot) + base_ref[...]
    base_ref[...] = base_ref[...] + jnp.sum(tot, axis=0, keepdims=True)

    qb = q_ref[0].astype(BF16)
    s = []
    for j in range(group):
        kb = k_refs[j][0, 0].reshape(w, F_HDIM).astype(BF16)
        bj = jnp.concatenate([bias[k * group + j:k * group + j + 1, :] for k in range(nblk)], axis=1)
        sj = lax.dot_general(qb, kb, _NT, preferred_element_type=F32) * scale + bj
        s.append(jnp.where(own, sj, -jnp.inf))
    s = jnp.concatenate(s, axis=1)
    m_prev = m_ref[...]
    m_new = jnp.maximum(m_prev, jnp.max(s, axis=1, keepdims=True))
    alpha = jnp.exp(m_prev - m_new)
    p = jnp.exp(s - m_new)
    l_ref[...] = alpha * l_ref[...] + jnp.sum(p, axis=1, keepdims=True)
    pv = None
    for j in range(group):
        pj = p[:, j * w:(j + 1) * w].astype(BF16)
        vb = v_refs[j][0, 0].reshape(w, F_HDIM).astype(BF16)
        d = jnp.dot(pj, vb, preferred_element_type=F32)
        pv = d if pv is None else pv + d
    acc_ref[...] = alpha * acc_ref[...] + pv
    m_ref[...] = m_new

    @pl.when(g == pl.num_programs(1) - 1)
    def _():
        o_ref[0] = acc_ref[...] / l_ref[...]


def _paged_call(page_table, q8, kn8, vn8, gs3, bias_row, cache_k, cache_v, lf_rows, group):
    b, n_pages = page_table.shape
    page = cache_k.shape[2]
    w = page * F_HEADS
    ng = n_pages // group
    m_blk, t_blk, u_row = _flat_consts(page, group)
    nrow = u_row.shape[0]

    def page_map5(j):
        return lambda bi, g, pt: (0, pt[bi, (ng - 1 - g) * group + j], 0, 0, 0)

    def page_map3(j):
        return lambda bi, g, pt: (pt[bi, (ng - 1 - g) * group + j], 0, 0)

    head = pl.BlockSpec((1, F_HEADS, F_HDIM), lambda bi, g, pt: (bi, 0, 0))
    const = lambda bi, g, pt: (0, 0)
    kv_spec = lambda j: pl.BlockSpec((1, 1, page, F_HEADS, F_HDIM), page_map5(j))
    grid_spec = pltpu.PrefetchScalarGridSpec(
        num_scalar_prefetch=1,
        grid=(b, ng),
        in_specs=(
            [head, head, head,
             pl.BlockSpec((1, 1, LANES), lambda bi, g, pt: (bi, 0, 0)),
             pl.BlockSpec((1, LANES), const),
             pl.BlockSpec((LANES, LANES), const),
             pl.BlockSpec((LANES, LANES), const),
             pl.BlockSpec((nrow, nrow), const)]
            + [kv_spec(j) for j in range(group)]
            + [kv_spec(j) for j in range(group)]
            + [pl.BlockSpec((1, 1, w), page_map3(j)) for j in range(group)]
        ),
        out_specs=pl.BlockSpec((1, F_HEADS, F_HDIM), lambda bi, g, pt: (bi, 0, 0)),
        scratch_shapes=[
            pltpu.VMEM((F_HEADS, 1), F32),
            pltpu.VMEM((F_HEADS, 1), F32),
            pltpu.VMEM((F_HEADS, F_HDIM), F32),
            pltpu.VMEM((1, LANES), F32),
        ],
    )
    return pl.pallas_call(
        functools.partial(_paged_kernel, group=group, scale=F_HDIM ** -0.5),
        grid_spec=grid_spec,
        out_shape=jax.ShapeDtypeStruct((b, F_HEADS, F_HDIM), F32),
        compiler_params=_cparams(("arbitrary", "arbitrary")),
        name="paged_attn",
    )(page_table, q8, kn8, vn8, gs3, bias_row, m_blk, t_blk, u_row,
      *([cache_k] * group), *([cache_v] * group), *([lf_rows] * group))


def _step_kernel(qk_ref, v_ref, o_ref, z_ref, fz_ref, gs_ref, b_ref, fo_ref, gain_ref,
                 c_ref, n_ref, m_ref,
                 c_out, n_out, m_out, lf_out, fp_out, mp_out, *, sb):
    g = gs_ref[...] + b_ref[...]
    ls = _log_sigmoid(g)
    lf_out[...] = ls[:, G_FOX:G_FOX + F_HEADS]
    fp_out[...] = _silu(fz_ref[...]) * fo_ref[...]
    first_row = lax.broadcasted_iota(jnp.int32, (CHUNK, 1), 0) == 0
    for b in range(sb):
        for h in range(M_HEADS):
            qs = slice(h * M_QKDIM, (h + 1) * M_QKDIM)
            ks = slice(M_QKW + h * M_QKDIM, M_QKW + (h + 1) * M_QKDIM)
            vs = slice(h * M_VDIM, (h + 1) * M_VDIM)
            q = qk_ref[b:b + 1, qs]
            k = qk_ref[b:b + 1, ks] * (M_QKDIM ** -0.5)
            v = v_ref[b:b + 1, vs]
            logi = g[b:b + 1, G_MI + h:G_MI + h + 1]
            logf = ls[b:b + 1, G_MF + h:G_MF + h + 1]
            m_prev = m_ref[b:b + 1, h:h + 1]
            inter = logf + m_prev
            m_t = jnp.maximum(inter, logi)
            dw = jnp.exp(logi - m_t)
            iw = jnp.exp(inter - m_t)
            c_old = c_ref[b, h]
            n_old = n_ref[b, h:h + 1, :]
            s = jnp.sum(q * k, axis=1, keepdims=True) * dw
            cq = lax.dot_general(q.astype(BF16), c_old.astype(BF16), _NT, preferred_element_type=F32)
            num = s * v + iw * cq
            den = s + iw * jnp.sum(q * n_old, axis=1, keepdims=True)
            hh = num / jnp.maximum(jnp.abs(den), jnp.exp(-m_t))
            v_pad = jnp.where(first_row, jnp.broadcast_to(v * dw, (CHUNK, M_VDIM)), 0.0).astype(BF16)
            k_pad = jnp.broadcast_to(k, (CHUNK, M_QKDIM)).astype(BF16)
            c_out[b, h] = iw * c_old + lax.dot_general(v_pad, k_pad, _TN, preferred_element_type=F32)
            n_out[b, h:h + 1, :] = iw * n_old + dw * k
            m_out[b:b + 1, h:h + 1] = m_t
            hn = hh * lax.rsqrt(jnp.mean(hh * hh, axis=1, keepdims=True) + EPS) * gain_ref[:, vs]
            mp_out[b:b + 1, vs] = _silu(z_ref[b:b + 1, vs]) * _sigmoid(o_ref[b:b + 1, vs]) * hn


def _step_call(ps, gs, bias_row, fo, gain, state_c, state_n, state_m, sb):
    b = ps.shape[0]
    col = lambda c: pl.BlockSpec((sb, COLB), lambda i: (i, c))
    const = lambda i: (0, 0)
    return pl.pallas_call(
        functools.partial(_step_kernel, sb=sb),
        grid=(b // sb,),
        in_specs=[
            col(4), col(5), col(6), col(7), col(3),
            pl.BlockSpec((sb, LANES), lambda i: (i, 0)),
            pl.BlockSpec((1, LANES), const),
            pl.BlockSpec((sb, F_WIDTH), lambda i: (i, 0)),
            pl.BlockSpec((1, M_WIDTH), const),
            pl.BlockSpec((sb, M_HEADS, M_VDIM, M_QKDIM), lambda i: (i, 0, 0, 0)),
            pl.BlockSpec((sb, M_HEADS, M_QKDIM), lambda i: (i, 0, 0)),
            pl.BlockSpec((sb, M_HEADS), lambda i: (i, 0)),
        ],
        out_specs=[
            pl.BlockSpec((sb, M_HEADS, M_VDIM, M_QKDIM), lambda i: (i, 0, 0, 0)),
            pl.BlockSpec((sb, M_HEADS, M_QKDIM), lambda i: (i, 0, 0)),
            pl.BlockSpec((sb, M_HEADS), lambda i: (i, 0)),
            pl.BlockSpec((sb, F_HEADS), lambda i: (i, 0)),
            pl.BlockSpec((sb, F_WIDTH), lambda i: (i, 0)),
            pl.BlockSpec((sb, M_WIDTH), lambda i: (i, 0)),
        ],
        out_shape=[
            jax.ShapeDtypeStruct(state_c.shape, F32),
            jax.ShapeDtypeStruct(state_n.shape, F32),
            jax.ShapeDtypeStruct(state_m.shape, F32),
            jax.ShapeDtypeStruct((b, F_HEADS), F32),
            jax.ShapeDtypeStruct((b, F_WIDTH), F32),
            jax.ShapeDtypeStruct((b, M_WIDTH), F32),
        ],
        compiler_params=_cparams(("arbitrary",)),
        name="mlstm_step",
    )(ps, ps, ps, ps, ps, gs, bias_row, fo, gain, state_c, state_n, state_m)


def _pack_w_in(w):
    o = np.concatenate([[0], np.cumsum(PROJ_SIZES)])
    fq, fk, fv, ff, fz, mq, mk, mv, mi, mf, mo, mz = [w[:, o[i]:o[i + 1]] for i in range(12)]
    main = jnp.concatenate([fq, fk, fv, fz, mq, mk, mv, mo, mz], axis=1).astype(BF16)
    pad = jnp.zeros((w.shape[0], LANES - F_HEADS - 2 * M_HEADS), w.dtype)
    gate = jnp.concatenate([ff, mf, mi, pad], axis=1).astype(BF16)
    return main, gate


def kernel(x_prompt, x_sample, cache_k, cache_v, cache_logf, state_C, state_n, state_m,
           page_table, meta_tokens, norm_gain, w_in, b_fox_f, b_m_i, b_m_f, mh_gain,
           w_out, final_gain):
    batch, seq, d = x_prompt.shape
    depth = w_in.shape[0]
    assert batch == 1 and depth == 1 and x_sample.shape[1] == 1
    assert seq % CHUNK == 0
    s = x_sample.shape[0]
    n_real = N_META + seq
    lp = pl.cdiv(n_real, CHUNK) * CHUNK
    tile = _pick_tile(lp, (640, 512, 384, 256, 128))
    tile_out = _pick_tile(seq, (512, 256, 128))
    n_pool, page = cache_k.shape[1], cache_k.shape[2]
    n_pages = page_table.shape[1]
    assert page == CHUNK and n_pages % 8 == 0 and s % 8 == 0

    w_main, w_gate = _pack_w_in(w_in[0])
    w_o = w_out[0].astype(BF16)
    w_top, w_bot = w_o[:F_WIDTH], w_o[F_WIDTH:]
    bias_row = jnp.concatenate(
        [b_fox_f[0], b_m_f[0], b_m_i[0], jnp.zeros((LANES - F_HEADS - 2 * M_HEADS,), F32)])[None, :]
    gain = norm_gain[0][None, :]
    mh = mh_gain[0][None, :]
    fgain = final_gain[None, :]
    x2 = x_prompt[0]
    xs2 = x_sample[:, 0, :]

    xn, xsn = _norm_call(x2, meta_tokens, gain, xs2, lp, tile)
    qt_bf, k_out, k_bf, v_out, vt_bf, p2, gates, ps, gs = _inproj_call(xn, w_main, w_gate, xsn, n_real, tile)
    gc, ct, lf_p = _gates_call(gates, bias_row, n_real)

    fpart = _fox_call(qt_bf, k_bf, vt_bf, ct, gc, p2, tile)
    mpart, c_p, n_p, m_p = _mlstm_call(p2, gc, ct, mh)
    y_p = _out_call(x2, fpart, mpart, w_top, w_bot, fgain, tile_out)

    hd = (F_HEADS, F_HDIM)
    lf_rows = cache_logf.reshape(depth * n_pool, 1, page * F_HEADS)
    fo = _paged_call(page_table,
                     ps[:, 0:COLB].reshape(s, *hd), ps[:, COLB:2 * COLB].reshape(s, *hd),
                     ps[:, 2 * COLB:3 * COLB].reshape(s, *hd), gs.reshape(s, 1, LANES), bias_row,
                     cache_k, cache_v, lf_rows, group=8)
    c_s, n_s, m_s, lf_s, fps, mps = _step_call(ps, gs, bias_row, fo.reshape(s, F_WIDTH), mh,
                                               state_C[0], state_n[0], state_m[0], sb=8)
    y_s = _out_s_call(xs2, fps, mps, w_top, w_bot, fgain)

    return (
        y_p[None],
        y_s[:, None, :],
        k_out.reshape(1, 1, n_real, *hd),
        v_out.reshape(1, 1, n_real, *hd),
        lf_p.reshape(1, 1, n_real, F_HEADS),
        c_p[None, None],
        n_p[:, 0, :][None, None],
        m_p[:, 0, 0][None, None],
        ps[:, COLB:2 * COLB].reshape(1, s, 1, *hd),
        ps[:, 2 * COLB:3 * COLB].reshape(1, s, 1, *hd),
        lf_s.reshape(1, s, 1, F_HEADS),
        c_s[None],
        n_s[None],
        m_s[None],
    )
```

```python
import functools

import numpy as np
import jax
import jax.numpy as jnp
from jax import lax
from jax.experimental import pallas as pl
from jax.experimental.pallas import tpu as pltpu

F32 = jnp.float32
BF16 = jnp.bfloat16

N_META = 16
EPS = 1e-6
F_HEADS = 8
F_HDIM = 128
F_WIDTH = F_HEADS * F_HDIM
M_HEADS = 4
M_VDIM = 256
M_QKDIM = 128
M_WIDTH = M_HEADS * M_VDIM
M_QKW = M_HEADS * M_QKDIM
PROJ_SIZES = (F_WIDTH, F_WIDTH, F_WIDTH, F_HEADS, F_WIDTH,
              M_QKW, M_QKW, M_WIDTH, M_HEADS, M_HEADS, M_WIDTH, M_WIDTH)

LANES = 128
CHUNK = 128
COLB = 1024
N_COLB = 8
G_FOX = 0
G_MI = 8
G_MF = 12
NEG = -1e30
LOG2E = 1.4426950408889634
QSCALE = F_HDIM ** -0.5 * LOG2E
VMEM_LIMIT = 56 * 1024 * 1024

_NT = (((1,), (1,)), ((), ()))
_TN = (((0,), (0,)), ((), ()))


def _cparams(sem, vmem=VMEM_LIMIT):
    return pltpu.CompilerParams(dimension_semantics=sem, vmem_limit_bytes=vmem)


def _pick_tile(n, cands):
    for c in cands:
        if n % c == 0:
            return c
    raise ValueError(f"no tile for {n}")


def _log_sigmoid(x):
    return jnp.minimum(x, 0.0) - jnp.log1p(jnp.exp(-jnp.abs(x)))


def _sigmoid(x):
    return 1.0 / (1.0 + jnp.exp(-x))


def _silu(x):
    return x * _sigmoid(x)


def _split3(x):
    x1 = x.astype(BF16)
    r1 = x - x1.astype(F32)
    x2 = r1.astype(BF16)
    r2 = r1 - x2.astype(F32)
    x3 = r2.astype(BF16)
    return x1, x2, x3


def _dot3_l(a_bf, x):
    x1, x2, x3 = _split3(x)
    d = lambda v: jnp.dot(a_bf, v, preferred_element_type=F32)
    return (d(x3) + d(x2)) + d(x1)


def _dot3_r(x, b_bf):
    x1, x2, x3 = _split3(x)
    d = lambda v: jnp.dot(v, b_bf, preferred_element_type=F32)
    return (d(x3) + d(x2)) + d(x1)


def _norm_kernel(x_ref, prev_ref, meta_ref, g_ref, xs_ref, xn_ref, xsn_ref, *, tile, n_real):
    i = pl.program_id(0)
    g = g_ref[...]

    def nrm(x):
        return x * lax.rsqrt(jnp.mean(x * x, axis=-1, keepdims=True) + EPS) * g

    top = jnp.where(i == 0, meta_ref[...], prev_ref[...])
    xn_ref[0:N_META, :] = nrm(top).astype(BF16)
    body = x_ref[0:tile - N_META, :]
    row = i * tile + N_META + lax.broadcasted_iota(jnp.int32, (tile - N_META, 1), 0)
    xn_ref[N_META:tile, :] = jnp.where(row < n_real, nrm(body), 0.0).astype(BF16)

    @pl.when(i == 0)
    def _():
        xsn_ref[...] = nrm(xs_ref[...]).astype(BF16)


def _norm_call(x, meta, gain, xs, lp, tile):
    seq, d = x.shape
    n_real = seq + N_META
    nblk = pl.cdiv(seq, tile)
    per = tile // N_META
    nprev = seq // N_META
    s = xs.shape[0]
    return pl.pallas_call(
        functools.partial(_norm_kernel, tile=tile, n_real=n_real),
        grid=(lp // tile,),
        in_specs=[
            pl.BlockSpec((tile, d), lambda i: (jnp.minimum(i, nblk - 1), 0)),
            pl.BlockSpec((N_META, d), lambda i: (jnp.clip(i * per - 1, 0, nprev - 1), 0)),
            pl.BlockSpec((N_META, d), lambda i: (0, 0)),
            pl.BlockSpec((1, d), lambda i: (0, 0)),
            pl.BlockSpec((s, d), lambda i: (0, 0)),
        ],
        out_specs=[
            pl.BlockSpec((tile, d), lambda i: (i, 0)),
            pl.BlockSpec((s, d), lambda i: (0, 0)),
        ],
        out_shape=[jax.ShapeDtypeStruct((lp, d), BF16), jax.ShapeDtypeStruct((s, d), BF16)],
        compiler_params=_cparams(("arbitrary",)),
        name="norm",
    )(x, x, meta, gain, xs)


def _inproj_kernel(x_ref, w_ref, wg_ref, xs_ref,
                   q_ref, ko_ref, kb_ref, vo_ref, vb_ref, p2_ref, g_ref, ps_ref, gs_ref):
    n = pl.program_id(0)
    m = pl.program_id(1)
    acc = jnp.dot(x_ref[...], w_ref[...], preferred_element_type=F32)

    @pl.when(n == 0)
    def _():
        q_ref[...] = (acc * QSCALE).T.astype(BF16)
        g_ref[...] = jnp.dot(x_ref[...], wg_ref[...], preferred_element_type=F32)

    @pl.when(n == 1)
    def _():
        ko_ref[...] = acc
        kb_ref[...] = acc.astype(BF16)

    @pl.when(n == 2)
    def _():
        vo_ref[...] = acc
        vb_ref[...] = acc.T.astype(BF16)

    @pl.when(n >= 3)
    def _():
        p2_ref[...] = acc

    @pl.when(m == 0)
    def _():
        ps_ref[...] = jnp.dot(xs_ref[...], w_ref[...], preferred_element_type=F32)

    @pl.when((m == 0) & (n == 0))
    def _():
        gs_ref[...] = jnp.dot(xs_ref[...], wg_ref[...], preferred_element_type=F32)


def _inproj_call(xn, w_main, w_gate, xsn, n_real, tile):
    lp, d = xn.shape
    s = xsn.shape[0]
    nm = lp // tile

    def held(n, m, lo, hi):
        return jnp.where(n < lo, 0, jnp.where(n > hi, nm - 1, m))

    def own(lo, hi):
        return lambda n, m: (held(n, m, lo, hi), 0)

    def own_t(lo, hi):
        return lambda n, m: (0, held(n, m, lo, hi))

    return pl.pallas_call(
        _inproj_kernel,
        grid=(N_COLB, nm),
        in_specs=[
            pl.BlockSpec((tile, d), lambda n, m: (m, 0)),
            pl.BlockSpec((d, COLB), lambda n, m: (0, n)),
            pl.BlockSpec((d, LANES), lambda n, m: (0, 0)),
            pl.BlockSpec((s, d), lambda n, m: (0, 0)),
        ],
        out_specs=[
            pl.BlockSpec((COLB, tile), own_t(0, 0)),
            pl.BlockSpec((tile, COLB), own(1, 1)),
            pl.BlockSpec((tile, COLB), own(1, 1)),
            pl.BlockSpec((tile, COLB), own(2, 2)),
            pl.BlockSpec((COLB, tile), own_t(2, 2)),
            pl.BlockSpec((tile, COLB), lambda n, m: (jnp.where(n < 3, 0, m), jnp.maximum(n - 3, 0))),
            pl.BlockSpec((tile, LANES), own(0, 0)),
            pl.BlockSpec((s, COLB), lambda n, m: (0, n)),
            pl.BlockSpec((s, LANES), lambda n, m: (0, 0)),
        ],
        out_shape=[
            jax.ShapeDtypeStruct((COLB, lp), BF16),
            jax.ShapeDtypeStruct((n_real, COLB), F32),
            jax.ShapeDtypeStruct((lp, COLB), BF16),
            jax.ShapeDtypeStruct((n_real, COLB), F32),
            jax.ShapeDtypeStruct((COLB, lp), BF16),
            jax.ShapeDtypeStruct((lp, (N_COLB - 3) * COLB), F32),
            jax.ShapeDtypeStruct((lp, LANES), F32),
            jax.ShapeDtypeStruct((s, N_COLB * COLB), F32),
            jax.ShapeDtypeStruct((s, LANES), F32),
        ],
        compiler_params=_cparams(("arbitrary", "arbitrary")),
        name="in_proj",
    )(xn, w_main, w_gate, xsn)


def _gates_kernel(g_ref, b_ref, ltri_ref, gc_ref, ct_ref, lf_ref, carry_ref, *, n_real):
    i = pl.program_id(0)

    @pl.when(i == 0)
    def _():
        carry_ref[...] = jnp.zeros_like(carry_ref)

    x = g_ref[...] + b_ref[...]
    lane = lax.broadcasted_iota(jnp.int32, (CHUNK, LANES), 1)
    row = i * CHUNK + lax.broadcasted_iota(jnp.int32, (CHUNK, LANES), 0)
    valid = row < n_real
    ls = _log_sigmoid(x)
    is_mf = (lane >= G_MF) & (lane < G_MF + M_HEADS)
    summed = jnp.where(valid & ((lane < F_HEADS) | is_mf), ls, 0.0)
    cs = _dot3_l(ltri_ref[...], summed)
    glob = cs + carry_ref[...]
    out = jnp.where(lane < F_HEADS, glob, jnp.where(is_mf, cs, jnp.where(valid, x, NEG)))
    gc_ref[...] = out
    ct_ref[...] = out.T[0:16, :]
    lf_ref[...] = ls[:, 0:F_HEADS]
    carry_ref[...] = glob[CHUNK - 1:CHUNK, :]


def _gates_call(gates, bias_row, n_real):
    lp = gates.shape[0]
    nb = lp // CHUNK
    ltri = jnp.asarray(np.tril(np.ones((CHUNK, CHUNK), np.float32)), BF16)
    return pl.pallas_call(
        functools.partial(_gates_kernel, n_real=n_real),
        grid=(nb,),
        in_specs=[
            pl.BlockSpec((CHUNK, LANES), lambda i: (i, 0)),
            pl.BlockSpec((1, LANES), lambda i: (0, 0)),
            pl.BlockSpec((CHUNK, CHUNK), lambda i: (0, 0)),
        ],
        out_specs=[
            pl.BlockSpec((CHUNK, LANES), lambda i: (i, 0)),
            pl.BlockSpec((16, CHUNK), lambda i: (0, i)),
            pl.BlockSpec((CHUNK, F_HEADS), lambda i: (i, 0)),
        ],
        out_shape=[
            jax.ShapeDtypeStruct((lp, LANES), F32),
            jax.ShapeDtypeStruct((16, lp), F32),
            jax.ShapeDtypeStruct((n_real, F_HEADS), F32),
        ],
        scratch_shapes=[pltpu.VMEM((1, LANES), F32)],
        compiler_params=_cparams(("arbitrary",)),
        name="gates",
    )(gates, bias_row, ltri)


def _fox_kernel(qi_ref, ki_ref, qt_ref, k_ref, vt_ref, gck_ref, ctq_ref, fz_ref, o_ref,
                acc_ref, m_ref, l_ref, t_ref):
    step_id = pl.program_id(0)
    qi = qi_ref[step_id]
    ki = ki_ref[step_id]
    tk = k_ref.shape[0]
    tq = qt_ref.shape[1]
    cols = [(c0, LANES) for c0 in range(0, tq, LANES)]

    @pl.when(ki == 0)
    def _():
        m_ref[...] = jnp.full_like(m_ref, -jnp.inf)
        l_ref[...] = jnp.zeros_like(l_ref)
        acc_ref[...] = jnp.zeros_like(acc_ref)

    def step(diag):
        for h in range(F_HEADS):
            hs = slice(h * F_HDIM, (h + 1) * F_HDIM)
            ck = gck_ref[:, G_FOX + h:G_FOX + h + 1] * LOG2E
            for c0, cw in cols:
                cs = slice(c0, c0 + cw)
                nk = c0 + cw if diag else tk
                s = jnp.dot(k_ref[0:nk, hs], qt_ref[hs, cs], preferred_element_type=F32)
                t = s - ck[0:nk, :]
                if diag:
                    keep = (lax.broadcasted_iota(jnp.int32, (nk, cw), 0)
                            <= c0 + lax.broadcasted_iota(jnp.int32, (nk, cw), 1))
                    t = jnp.where(keep, t, -jnp.inf)
                t_ref[0:nk, 0:cw] = t
                cq = ctq_ref[G_FOX + h:G_FOX + h + 1, cs] * LOG2E
                m_prev = m_ref[h, :, cs]
                m_new = jnp.maximum(m_prev, jnp.max(t, axis=0, keepdims=True) + cq)
                p = jnp.exp2(t_ref[0:nk, 0:cw] - (m_new - cq))
                alpha = jnp.exp2(m_prev - m_new)
                l_ref[h, :, cs] = alpha * l_ref[h, :, cs] + jnp.sum(p, axis=0, keepdims=True)
                acc_ref[hs, cs] = alpha * acc_ref[hs, cs] + jnp.dot(
                    vt_ref[hs, 0:nk], p.astype(BF16), preferred_element_type=F32)
                m_ref[h, :, cs] = m_new

    @pl.when(ki < qi)
    def _():
        step(False)

    @pl.when(ki == qi)
    def _():
        step(True)
        for h in range(F_HEADS):
            hs = slice(h * F_HDIM, (h + 1) * F_HDIM)
            for c in range(tq // LANES):
                cs = slice(c * LANES, (c + 1) * LANES)
                o = (acc_ref[hs, cs] / l_ref[h, :, cs]).T
                o_ref[cs, hs] = (_silu(fz_ref[cs, hs]) * o).astype(BF16)


def _fox_call(qt, k, vt, ct, gc, p2, tile):
    lp = k.shape[0]
    nq = lp // tile
    pairs = [(q, kk) for q in range(nq) for kk in range(q + 1)]
    qi_list = jnp.asarray(np.array([p[0] for p in pairs], np.int32))
    ki_list = jnp.asarray(np.array([p[1] for p in pairs], np.int32))
    grid_spec = pltpu.PrefetchScalarGridSpec(
        num_scalar_prefetch=2,
        grid=(len(pairs),),
        in_specs=[
            pl.BlockSpec((F_WIDTH, tile), lambda s, qi, ki: (0, qi[s])),
            pl.BlockSpec((tile, F_WIDTH), lambda s, qi, ki: (ki[s], 0)),
            pl.BlockSpec((F_WIDTH, tile), lambda s, qi, ki: (0, ki[s])),
            pl.BlockSpec((tile, LANES), lambda s, qi, ki: (ki[s], 0)),
            pl.BlockSpec((16, tile), lambda s, qi, ki: (0, qi[s])),
            pl.BlockSpec((tile, COLB), lambda s, qi, ki: (qi[s], 0)),
        ],
        out_specs=pl.BlockSpec((tile, F_WIDTH), lambda s, qi, ki: (qi[s], 0)),
        scratch_shapes=[
            pltpu.VMEM((F_WIDTH, tile), F32),
            pltpu.VMEM((F_HEADS, 1, tile), F32),
            pltpu.VMEM((F_HEADS, 1, tile), F32),
            pltpu.VMEM((tile, LANES), F32),
        ],
    )
    return pl.pallas_call(
        _fox_kernel,
        grid_spec=grid_spec,
        out_shape=jax.ShapeDtypeStruct((lp, F_WIDTH), BF16),
        compiler_params=_cparams(("arbitrary",)),
        name="fox",
    )(qi_list, ki_list, qt, k, vt, gc, ct, p2)


def _mlstm_kernel(qk_ref, v_ref, o_ref, z_ref, gc_ref, ct_ref, gain_ref,
                  mp_ref, c_out, n_out, m_out, c_s, n_s, m_s):
    i = pl.program_id(0)

    @pl.when(i == 0)
    def _():
        c_s[...] = jnp.zeros_like(c_s)
        n_s[...] = jnp.zeros_like(n_s)
        m_s[...] = jnp.zeros_like(m_s)

    causal = (lax.broadcasted_iota(jnp.int32, (CHUNK, CHUNK), 0)
              >= lax.broadcasted_iota(jnp.int32, (CHUNK, CHUNK), 1))
    for h in range(M_HEADS):
        qs = slice(h * M_QKDIM, (h + 1) * M_QKDIM)
        ks = slice(M_QKW + h * M_QKDIM, M_QKW + (h + 1) * M_QKDIM)
        vs = slice(h * M_VDIM, (h + 1) * M_VDIM)
        q = qk_ref[:, qs]
        k = qk_ref[:, ks] * (M_QKDIM ** -0.5)
        v = v_ref[:, vs]
        bt_c = gc_ref[:, G_MF + h:G_MF + h + 1]
        it_c = gc_ref[:, G_MI + h:G_MI + h + 1]
        bt_r = ct_ref[G_MF + h:G_MF + h + 1, :]
        it_r = ct_ref[G_MI + h:G_MI + h + 1, :]
        m_prev = m_s[h][:, 0:1]
        dlog = jnp.where(causal, bt_c - bt_r + it_r, -jnp.inf)
        inter = bt_c + m_prev
        m_t = jnp.maximum(inter, jnp.max(dlog, axis=1, keepdims=True))
        dw = jnp.exp(dlog - m_t)
        iw = jnp.exp(inter - m_t)
        qb = q.astype(BF16)
        kb = k.astype(BF16)
        s = lax.dot_general(qb, kb, _NT, preferred_element_type=F32) * dw
        c_old = c_s[h]
        n_old = n_s[h]
        num = (jnp.dot(s.astype(BF16), v.astype(BF16), preferred_element_type=F32)
               + iw * lax.dot_general(qb, c_old.astype(BF16), _NT, preferred_element_type=F32))
        den = jnp.sum(s, axis=1, keepdims=True) + iw * jnp.sum(q * n_old, axis=1, keepdims=True)
        hh = num / jnp.maximum(jnp.abs(den), jnp.exp(-m_t))
        m_new = m_t[CHUNK - 1:CHUNK, :]
        bt_last = bt_c[CHUNK - 1:CHUNK, :]
        wl_c = jnp.exp(bt_last - bt_c + it_c - m_new)
        decay = jnp.exp(bt_last + m_prev - m_new)
        vw = (v * wl_c).astype(BF16)
        c_s[h] = decay * c_old + lax.dot_general(vw, kb, _TN, preferred_element_type=F32)
        n_s[h] = decay * n_old + jnp.sum(wl_c * k, axis=0, keepdims=True)
        m_s[h] = jnp.broadcast_to(m_new, (1, LANES))
        hn = hh * lax.rsqrt(jnp.mean(hh * hh, axis=1, keepdims=True) + EPS) * gain_ref[:, vs]
        mp_ref[:, vs] = (_silu(z_ref[:, vs]) * _sigmoid(o_ref[:, vs]) * hn).astype(BF16)

    @pl.when(i == pl.num_programs(0) - 1)
    def _():
        c_out[...] = c_s[...]
        n_out[...] = n_s[...]
        m_out[...] = m_s[...]


def _mlstm_call(p2, gc, ct, gain):
    lp = p2.shape[0]
    nb = lp // CHUNK
    const3 = lambda i: (0, 0, 0)
    return pl.pallas_call(
        _mlstm_kernel,
        grid=(nb,),
        in_specs=[
            pl.BlockSpec((CHUNK, COLB), lambda i: (i, 1)),
            pl.BlockSpec((CHUNK, COLB), lambda i: (i, 2)),
            pl.BlockSpec((CHUNK, COLB), lambda i: (i, 3)),
            pl.BlockSpec((CHUNK, COLB), lambda i: (i, 4)),
            pl.BlockSpec((CHUNK, LANES), lambda i: (i, 0)),
            pl.BlockSpec((16, CHUNK), lambda i: (0, i)),
            pl.BlockSpec((1, M_WIDTH), lambda i: (0, 0)),
        ],
        out_specs=[
            pl.BlockSpec((CHUNK, M_WIDTH), lambda i: (i, 0)),
            pl.BlockSpec((M_HEADS, M_VDIM, M_QKDIM), const3),
            pl.BlockSpec((M_HEADS, 1, M_QKDIM), const3),
            pl.BlockSpec((M_HEADS, 1, LANES), const3),
        ],
        out_shape=[
            jax.ShapeDtypeStruct((lp, M_WIDTH), BF16),
            jax.ShapeDtypeStruct((M_HEADS, M_VDIM, M_QKDIM), F32),
            jax.ShapeDtypeStruct((M_HEADS, 1, M_QKDIM), F32),
            jax.ShapeDtypeStruct((M_HEADS, 1, LANES), F32),
        ],
        scratch_shapes=[
            pltpu.VMEM((M_HEADS, M_VDIM, M_QKDIM), F32),
            pltpu.VMEM((M_HEADS, 1, M_QKDIM), F32),
            pltpu.VMEM((M_HEADS, 1, LANES), F32),
        ],
        compiler_params=_cparams(("arbitrary",)),
        name="mlstm",
    )(p2, p2, p2, p2, gc, ct, gain)


def _out_kernel(x_ref, fm_ref, fn_ref, mm_ref, mn_ref, wt_ref, wb_ref, g_ref, y_ref):
    t = x_ref.shape[0]
    cat_f = jnp.concatenate([fm_ref[N_META:t, :], fn_ref[...]], axis=0)
    cat_m = jnp.concatenate([mm_ref[N_META:t, :], mn_ref[...]], axis=0)
    y = (x_ref[...]
         + jnp.dot(cat_f, wt_ref[...], preferred_element_type=F32)
         + jnp.dot(cat_m, wb_ref[...], preferred_element_type=F32))
    y_ref[...] = y * lax.rsqrt(jnp.mean(y * y, axis=-1, keepdims=True) + EPS) * g_ref[...]


def _out_call(x, fpart, mpart, w_top, w_bot, gain, tile):
    seq, d = x.shape
    per = tile // N_META
    main = lambda i: (i, 0)
    nxt = lambda i: ((i + 1) * per, 0)
    const = lambda i: (0, 0)
    return pl.pallas_call(
        _out_kernel,
        grid=(seq // tile,),
        in_specs=[
            pl.BlockSpec((tile, d), main),
            pl.BlockSpec((tile, F_WIDTH), main),
            pl.BlockSpec((N_META, F_WIDTH), nxt),
            pl.BlockSpec((tile, M_WIDTH), main),
            pl.BlockSpec((N_META, M_WIDTH), nxt),
            pl.BlockSpec((F_WIDTH, d), const),
            pl.BlockSpec((M_WIDTH, d), const),
            pl.BlockSpec((1, d), const),
        ],
        out_specs=pl.BlockSpec((tile, d), main),
        out_shape=jax.ShapeDtypeStruct((seq, d), F32),
        compiler_params=_cparams(("arbitrary",)),
        name="out_proj",
    )(x, fpart, fpart, mpart, mpart, w_top, w_bot, gain)


def _out_s_kernel(x_ref, f_ref, m_ref, wt_ref, wb_ref, g_ref, y_ref):
    y = (x_ref[...]
         + jnp.dot(f_ref[...].astype(BF16), wt_ref[...], preferred_element_type=F32)
         + jnp.dot(m_ref[...].astype(BF16), wb_ref[...], preferred_element_type=F32))
    y_ref[...] = y * lax.rsqrt(jnp.mean(y * y, axis=-1, keepdims=True) + EPS) * g_ref[...]


def _out_s_call(xs, fps, mps, w_top, w_bot, gain):
    s, d = xs.shape
    return pl.pallas_call(
        _out_s_kernel,
        out_shape=jax.ShapeDtypeStruct((s, d), F32),
        compiler_params=pltpu.CompilerParams(vmem_limit_bytes=VMEM_LIMIT),
        name="out_proj_s",
    )(xs, fps, mps, w_top, w_bot, gain)


def _flat_consts(page, group):
    lane = np.arange(LANES)
    b, h = lane // F_HEADS, lane % F_HEADS
    same = h[:, None] == h[None, :]
    m_blk = (same & (b[:, None] > b[None, :])).astype(np.float32)
    t_blk = same.astype(np.float32)
    r = np.arange(group * (page * F_HEADS // LANES))
    u_row = r[None, :] > r[:, None]
    return (jnp.asarray(m_blk, BF16), jnp.asarray(t_blk, BF16), jnp.asarray(u_row.astype(np.float32), BF16))


def _paged_kernel(pt_ref, q_ref, kn_ref, vn_ref, gs_ref, b_ref, mb_ref, tb_ref, u_ref, *rest,
                  group, scale):
    k_refs = rest[:group]
    v_refs = rest[group:2 * group]
    lf_refs = rest[2 * group:3 * group]
    o_ref, m_ref, l_ref, acc_ref, base_ref = rest[3 * group:]
    g = pl.program_id(1)
    nblk = lf_refs[0].shape[1]
    w = nblk * LANES
    own = (lax.broadcasted_iota(jnp.int32, (F_HEADS, w), 1) % F_HEADS
           == lax.broadcasted_iota(jnp.int32, (F_HEADS, w), 0))

    @pl.when(g == 0)
    def _():
        m_ref[...] = jnp.sum(q_ref[0] * kn_ref[0], axis=1, keepdims=True) * scale
        l_ref[...] = jnp.ones_like(l_ref)
        acc_ref[...] = vn_ref[0]
        cn = _log_sigmoid(gs_ref[0] + b_ref[...])
        head_lane = lax.broadcasted_iota(jnp.int32, (1, LANES), 1) < F_HEADS
        base_ref[...] = _dot3_r(jnp.where(head_lane, cn, 0.0), tb_ref[...])

    xs = jnp.concatenate([r[0] for r in lf_refs], axis=0)
    tot = _dot3_r(xs, tb_ref[...])
    bias = _dot3_r(xs, mb_ref[...]) + _dot3_l(u_ref[...], tot) + base_ref[...]
    base_ref[...] = base_ref[...] + jnp.sum(tot, axis=0, keepdims=True)

    qb = q_ref[0].astype(BF16)
    s = []
    for j in range(group):
        kb = k_refs[j][0, 0].reshape(w, F_HDIM).astype(BF16)
        bj = jnp.concatenate([bias[j * nblk + k:j * nblk + k + 1, :] for k in range(nblk)], axis=1)
        sj = lax.dot_general(qb, kb, _NT, preferred_element_type=F32) * scale + bj
        s.append(jnp.where(own, sj, -jnp.inf))
    s = jnp.concatenate(s, axis=1)
    m_prev = m_ref[...]
    m_new = jnp.maximum(m_prev, jnp.max(s, axis=1, keepdims=True))
    alpha = jnp.exp(m_prev - m_new)
    p = jnp.exp(s - m_new)
    l_ref[...] = alpha * l_ref[...] + jnp.sum(p, axis=1, keepdims=True)
    pv = None
    for j in range(group):
        pj = p[:, j * w:(j + 1) * w].astype(BF16)
        vb = v_refs[j][0, 0].reshape(w, F_HDIM).astype(BF16)
        d = jnp.dot(pj, vb, preferred_element_type=F32)
        pv = d if pv is None else pv + d
    acc_ref[...] = alpha * acc_ref[...] + pv
    m_ref[...] = m_new

    @pl.when(g == pl.num_programs(1) - 1)
    def _():
        o_ref[0] = acc_ref[...] / l_ref[...]


def _paged_call(page_table, q8, kn8, vn8, gs3, bias_row, cache_k, cache_v, lf_rows, group):
    b, n_pages = page_table.shape
    page = cache_k.shape[2]
    w = page * F_HEADS
    ng = n_pages // group
    m_blk, t_blk, u_row = _flat_consts(page, group)
    nrow = u_row.shape[0]

    def page_map5(j):
        return lambda bi, g, pt: (0, pt[bi, (ng - 1 - g) * group + j], 0, 0, 0)

    def page_map3(j):
        return lambda bi, g, pt: (pt[bi, (ng - 1 - g) * group + j], 0, 0)

    head = pl.BlockSpec((1, F_HEADS, F_HDIM), lambda bi, g, pt: (bi, 0, 0))
    const = lambda bi, g, pt: (0, 0)
    kv_spec = lambda j: pl.BlockSpec((1, 1, page, F_HEADS, F_HDIM), page_map5(j))
    grid_spec = pltpu.PrefetchScalarGridSpec(
        num_scalar_prefetch=1,
        grid=(b, ng),
        in_specs=(
            [head, head, head,
             pl.BlockSpec((1, 1, LANES), lambda bi, g, pt: (bi, 0, 0)),
             pl.BlockSpec((1, LANES), const),
             pl.BlockSpec((LANES, LANES), const),
             pl.BlockSpec((LANES, LANES), const),
             pl.BlockSpec((nrow, nrow), const)]
            + [kv_spec(j) for j in range(group)]
            + [kv_spec(j) for j in range(group)]
            + [pl.BlockSpec((1, w // LANES, LANES), page_map3(j)) for j in range(group)]
        ),
        out_specs=pl.BlockSpec((1, F_HEADS, F_HDIM), lambda bi, g, pt: (bi, 0, 0)),
        scratch_shapes=[
            pltpu.VMEM((F_HEADS, 1), F32),
            pltpu.VMEM((F_HEADS, 1), F32),
            pltpu.VMEM((F_HEADS, F_HDIM), F32),
            pltpu.VMEM((1, LANES), F32),
        ],
    )
    return pl.pallas_call(
        functools.partial(_paged_kernel, group=group, scale=F_HDIM ** -0.5),
        grid_spec=grid_spec,
        out_shape=jax.ShapeDtypeStruct((b, F_HEADS, F_HDIM), F32),
        compiler_params=_cparams(("arbitrary", "arbitrary")),
        name="paged_attn",
    )(page_table, q8, kn8, vn8, gs3, bias_row, m_blk, t_blk, u_row,
      *([cache_k] * group), *([cache_v] * group), *([lf_rows] * group))


def _step_kernel(qk_ref, v_ref, o_ref, z_ref, fz_ref, gs_ref, b_ref, fo_ref, gain_ref,
                 c_ref, n_ref, m_ref,
                 c_out, n_out, m_out, lf_out, fp_out, mp_out, *, sb):
    g = gs_ref[...] + b_ref[...]
    ls = _log_sigmoid(g)
    lf_out[...] = ls[:, G_FOX:G_FOX + F_HEADS]
    fp_out[...] = _silu(fz_ref[...]) * fo_ref[...]
    first_row = lax.broadcasted_iota(jnp.int32, (CHUNK, 1), 0) == 0
    for b in range(sb):
        for h in range(M_HEADS):
            qs = slice(h * M_QKDIM, (h + 1) * M_QKDIM)
            ks = slice(M_QKW + h * M_QKDIM, M_QKW + (h + 1) * M_QKDIM)
            vs = slice(h * M_VDIM, (h + 1) * M_VDIM)
            q = qk_ref[b:b + 1, qs]
            k = qk_ref[b:b + 1, ks] * (M_QKDIM ** -0.5)
            v = v_ref[b:b + 1, vs]
            logi = g[b:b + 1, G_MI + h:G_MI + h + 1]
            logf = ls[b:b + 1, G_MF + h:G_MF + h + 1]
            m_prev = m_ref[b:b + 1, h:h + 1]
            inter = logf + m_prev
            m_t = jnp.maximum(inter, logi)
            dw = jnp.exp(logi - m_t)
            iw = jnp.exp(inter - m_t)
            c_old = c_ref[b, h]
            n_old = n_ref[b, h:h + 1, :]
            s = jnp.sum(q * k, axis=1, keepdims=True) * dw
            cq = lax.dot_general(q.astype(BF16), c_old.astype(BF16), _NT, preferred_element_type=F32)
            num = s * v + iw * cq
            den = s + iw * jnp.sum(q * n_old, axis=1, keepdims=True)
            hh = num / jnp.maximum(jnp.abs(den), jnp.exp(-m_t))
            v_pad = jnp.where(first_row, jnp.broadcast_to(v * dw, (CHUNK, M_VDIM)), 0.0).astype(BF16)
            k_pad = jnp.broadcast_to(k, (CHUNK, M_QKDIM)).astype(BF16)
            c_out[b, h] = iw * c_old + lax.dot_general(v_pad, k_pad, _TN, preferred_element_type=F32)
            n_out[b, h:h + 1, :] = iw * n_old + dw * k
            m_out[b:b + 1, h:h + 1] = m_t
            hn = hh * lax.rsqrt(jnp.mean(hh * hh, axis=1, keepdims=True) + EPS) * gain_ref[:, vs]
            mp_out[b:b + 1, vs] = _silu(z_ref[b:b + 1, vs]) * _sigmoid(o_ref[b:b + 1, vs]) * hn


def _step_call(ps, gs, bias_row, fo, gain, state_c, state_n, state_m, sb):
    b = ps.shape[0]
    col = lambda c: pl.BlockSpec((sb, COLB), lambda i: (i, c))
    const = lambda i: (0, 0)
    return pl.pallas_call(
        functools.partial(_step_kernel, sb=sb),
        grid=(b // sb,),
        in_specs=[
            col(4), col(5), col(6), col(7), col(3),
            pl.BlockSpec((sb, LANES), lambda i: (i, 0)),
            pl.BlockSpec((1, LANES), const),
            pl.BlockSpec((sb, F_WIDTH), lambda i: (i, 0)),
            pl.BlockSpec((1, M_WIDTH), const),
            pl.BlockSpec((sb, M_HEADS, M_VDIM, M_QKDIM), lambda i: (i, 0, 0, 0)),
            pl.BlockSpec((sb, M_HEADS, M_QKDIM), lambda i: (i, 0, 0)),
            pl.BlockSpec((sb, M_HEADS), lambda i: (i, 0)),
        ],
        out_specs=[
            pl.BlockSpec((sb, M_HEADS, M_VDIM, M_QKDIM), lambda i: (i, 0, 0, 0)),
            pl.BlockSpec((sb, M_HEADS, M_QKDIM), lambda i: (i, 0, 0)),
            pl.BlockSpec((sb, M_HEADS), lambda i: (i, 0)),
            pl.BlockSpec((sb, F_HEADS), lambda i: (i, 0)),
            pl.BlockSpec((sb, F_WIDTH), lambda i: (i, 0)),
            pl.BlockSpec((sb, M_WIDTH), lambda i: (i, 0)),
        ],
        out_shape=[
            jax.ShapeDtypeStruct(state_c.shape, F32),
            jax.ShapeDtypeStruct(state_n.shape, F32),
            jax.ShapeDtypeStruct(state_m.shape, F32),
            jax.ShapeDtypeStruct((b, F_HEADS), F32),
            jax.ShapeDtypeStruct((b, F_WIDTH), F32),
            jax.ShapeDtypeStruct((b, M_WIDTH), F32),
        ],
        compiler_params=_cparams(("arbitrary",)),
        name="mlstm_step",
    )(ps, ps, ps, ps, ps, gs, bias_row, fo, gain, state_c, state_n, state_m)


_COL_START = np.concatenate([[0], np.cumsum(PROJ_SIZES)])


def _repack_kernel(w_ref, wp_ref, wg_ref):
    o = _COL_START
    for n, a in enumerate((o[0], o[1], o[2], o[4], o[5], o[7], o[10], o[11])):
        wp_ref[:, n * COLB:(n + 1) * COLB] = w_ref[:, a:a + COLB].astype(BF16)
    assert o[3] % LANES == G_FOX and o[8] % LANES == G_MI and o[9] % LANES == G_MF
    t_ff = w_ref[:, o[3] - G_FOX:o[3] - G_FOX + LANES]
    t_m = w_ref[:, o[8] - G_MI:o[8] - G_MI + LANES]
    lane = lax.broadcasted_iota(jnp.int32, t_ff.shape, 1)
    wg = jnp.where(lane < F_HEADS, t_ff, jnp.where(lane < G_MF + M_HEADS, t_m, 0.0))
    wg_ref[...] = wg.astype(BF16)


def _repack_call(w):
    d, dp = w.shape
    tr = _pick_tile(d, (256, 128, 8))
    return pl.pallas_call(
        _repack_kernel,
        grid=(d // tr,),
        in_specs=[pl.BlockSpec((tr, dp), lambda i: (i, 0))],
        out_specs=[pl.BlockSpec((tr, N_COLB * COLB), lambda i: (i, 0)),
                   pl.BlockSpec((tr, LANES), lambda i: (i, 0))],
        out_shape=[jax.ShapeDtypeStruct((d, N_COLB * COLB), BF16),
                   jax.ShapeDtypeStruct((d, LANES), BF16)],
        compiler_params=_cparams(("arbitrary",)),
        name="repack",
    )(w)


def kernel(x_prompt, x_sample, cache_k, cache_v, cache_logf, state_C, state_n, state_m,
           page_table, meta_tokens, norm_gain, w_in, b_fox_f, b_m_i, b_m_f, mh_gain,
           w_out, final_gain):
    batch, seq, d = x_prompt.shape
    depth = w_in.shape[0]
    assert batch == 1 and depth == 1 and x_sample.shape[1] == 1
    assert seq % CHUNK == 0
    s = x_sample.shape[0]
    n_real = N_META + seq
    lp = pl.cdiv(n_real, CHUNK) * CHUNK
    tile = _pick_tile(lp, (640, 512, 384, 256, 128))
    tile_out = _pick_tile(seq, (512, 256, 128))
    n_pool, page = cache_k.shape[1], cache_k.shape[2]
    n_pages = page_table.shape[1]
    assert page == CHUNK and n_pages % 8 == 0 and s % 8 == 0

    w_main, w_gate = _repack_call(w_in[0])
    w_o = w_out[0].astype(BF16)
    w_top, w_bot = w_o[:F_WIDTH], w_o[F_WIDTH:]
    bias_row = jnp.concatenate(
        [b_fox_f[0], b_m_i[0], b_m_f[0], jnp.zeros((LANES - F_HEADS - 2 * M_HEADS,), F32)])[None, :]
    gain = norm_gain[0][None, :]
    mh = mh_gain[0][None, :]
    fgain = final_gain[None, :]
    x2 = x_prompt[0]
    xs2 = x_sample[:, 0, :]

    xn, xsn = _norm_call(x2, meta_tokens, gain, xs2, lp, tile)
    qt_bf, k_out, k_bf, v_out, vt_bf, p2, gates, ps, gs = _inproj_call(xn, w_main, w_gate, xsn, n_real, tile)
    gc, ct, lf_p = _gates_call(gates, bias_row, n_real)

    fpart = _fox_call(qt_bf, k_bf, vt_bf, ct, gc, p2, tile)
    mpart, c_p, n_p, m_p = _mlstm_call(p2, gc, ct, mh)
    y_p = _out_call(x2, fpart, mpart, w_top, w_bot, fgain, tile_out)

    hd = (F_HEADS, F_HDIM)
    lf_rows = cache_logf.reshape(depth * n_pool, page * F_HEADS // LANES, LANES)
    fo = _paged_call(page_table,
                     ps[:, 0:COLB].reshape(s, *hd), ps[:, COLB:2 * COLB].reshape(s, *hd),
                     ps[:, 2 * COLB:3 * COLB].reshape(s, *hd), gs.reshape(s, 1, LANES), bias_row,
                     cache_k, cache_v, lf_rows, group=8)
    c_s, n_s, m_s, lf_s, fps, mps = _step_call(ps, gs, bias_row, fo.reshape(s, F_WIDTH), mh,
                                               state_C[0], state_n[0], state_m[0], sb=8)
    y_s = _out_s_call(xs2, fps, mps, w_top, w_bot, fgain)

    return (
        y_p[None],
        y_s[:, None, :],
        k_out.reshape(1, 1, n_real, *hd),
        v_out.reshape(1, 1, n_real, *hd),
        lf_p.reshape(1, 1, n_real, F_HEADS),
        c_p[None, None],
        n_p[:, 0, :][None, None],
        m_p[:, 0, 0][None, None],
        ps[:, COLB:2 * COLB].reshape(1, s, 1, *hd),
        ps[:, 2 * COLB:3 * COLB].reshape(1, s, 1, *hd),
        lf_s.reshape(1, s, 1, F_HEADS),
        c_s[None],
        n_s[None],
        m_s[None],
    )
```

```python
import functools

import numpy as np
import jax
import jax.numpy as jnp
from jax import lax
from jax.experimental import pallas as pl
from jax.experimental.pallas import tpu as pltpu

F32 = jnp.float32
BF16 = jnp.bfloat16

N_META = 16
EPS = 1e-6
F_HEADS = 8
F_HDIM = 128
F_WIDTH = F_HEADS * F_HDIM
M_HEADS = 4
M_VDIM = 256
M_QKDIM = 128
M_WIDTH = M_HEADS * M_VDIM
M_QKW = M_HEADS * M_QKDIM
PROJ_SIZES = (F_WIDTH, F_WIDTH, F_WIDTH, F_HEADS, F_WIDTH,
              M_QKW, M_QKW, M_WIDTH, M_HEADS, M_HEADS, M_WIDTH, M_WIDTH)

LANES = 128
CHUNK = 128
COLB = 1024
N_COLB = 8
G_FOX = 0
G_MI = 8
G_MF = 12
NEG = -1e30
LOG2E = 1.4426950408889634
QSCALE = F_HDIM ** -0.5 * LOG2E
VMEM_LIMIT = 56 * 1024 * 1024
KV_SLOTS = 3

_NT = (((1,), (1,)), ((), ()))
_TN = (((0,), (0,)), ((), ()))


def _cparams(sem, vmem=VMEM_LIMIT):
    return pltpu.CompilerParams(dimension_semantics=sem, vmem_limit_bytes=vmem)


def _pick_tile(n, cands):
    for c in cands:
        if n % c == 0:
            return c
    raise ValueError(f"no tile for {n}")


def _log_sigmoid(x):
    return jnp.minimum(x, 0.0) - jnp.log1p(jnp.exp(-jnp.abs(x)))


def _sigmoid(x):
    return 1.0 / (1.0 + jnp.exp(-x))


def _silu(x):
    return x * _sigmoid(x)


def _split3(x):
    x1 = x.astype(BF16)
    r1 = x - x1.astype(F32)
    x2 = r1.astype(BF16)
    r2 = r1 - x2.astype(F32)
    x3 = r2.astype(BF16)
    return x1, x2, x3


def _dot3_l(a_bf, x):
    x1, x2, x3 = _split3(x)
    d = lambda v: jnp.dot(a_bf, v, preferred_element_type=F32)
    return (d(x3) + d(x2)) + d(x1)


def _dot3_r(x, b_bf):
    x1, x2, x3 = _split3(x)
    d = lambda v: jnp.dot(v, b_bf, preferred_element_type=F32)
    return (d(x3) + d(x2)) + d(x1)


def _norm_kernel(x_ref, prev_ref, meta_ref, g_ref, xs_ref, xn_ref, xsn_ref, *, tile, n_real):
    i = pl.program_id(0)
    g = g_ref[...]

    def nrm(x):
        return x * lax.rsqrt(jnp.mean(x * x, axis=-1, keepdims=True) + EPS) * g

    top = jnp.where(i == 0, meta_ref[...], prev_ref[...])
    xn_ref[0:N_META, :] = nrm(top).astype(BF16)
    body = x_ref[0:tile - N_META, :]
    row = i * tile + N_META + lax.broadcasted_iota(jnp.int32, (tile - N_META, 1), 0)
    xn_ref[N_META:tile, :] = jnp.where(row < n_real, nrm(body), 0.0).astype(BF16)

    @pl.when(i == 0)
    def _():
        xsn_ref[...] = nrm(xs_ref[...]).astype(BF16)


def _norm_call(x, meta, gain, xs, lp, tile):
    seq, d = x.shape
    n_real = seq + N_META
    nblk = pl.cdiv(seq, tile)
    per = tile // N_META
    nprev = seq // N_META
    s = xs.shape[0]
    return pl.pallas_call(
        functools.partial(_norm_kernel, tile=tile, n_real=n_real),
        grid=(lp // tile,),
        in_specs=[
            pl.BlockSpec((tile, d), lambda i: (jnp.minimum(i, nblk - 1), 0)),
            pl.BlockSpec((N_META, d), lambda i: (jnp.clip(i * per - 1, 0, nprev - 1), 0)),
            pl.BlockSpec((N_META, d), lambda i: (0, 0)),
            pl.BlockSpec((1, d), lambda i: (0, 0)),
            pl.BlockSpec((s, d), lambda i: (0, 0)),
        ],
        out_specs=[
            pl.BlockSpec((tile, d), lambda i: (i, 0)),
            pl.BlockSpec((s, d), lambda i: (0, 0)),
        ],
        out_shape=[jax.ShapeDtypeStruct((lp, d), BF16), jax.ShapeDtypeStruct((s, d), BF16)],
        compiler_params=_cparams(("arbitrary",)),
        name="norm",
    )(x, x, meta, gain, xs)


def _inproj_kernel(x_ref, w_ref, wg_ref, xs_ref,
                   q_ref, ko_ref, kb_ref, vo_ref, vb_ref, p2_ref, g_ref, ps_ref, gs_ref):
    n = pl.program_id(0)
    m = pl.program_id(1)
    acc = jnp.dot(x_ref[...], w_ref[...], preferred_element_type=F32)

    @pl.when(n == 0)
    def _():
        q_ref[...] = (acc * QSCALE).T.astype(BF16)
        g_ref[...] = jnp.dot(x_ref[...], wg_ref[...], preferred_element_type=F32)

    @pl.when(n == 1)
    def _():
        ko_ref[...] = acc
        kb_ref[...] = acc.astype(BF16)

    @pl.when(n == 2)
    def _():
        vo_ref[...] = acc
        vb_ref[...] = acc.T.astype(BF16)

    @pl.when(n >= 3)
    def _():
        p2_ref[...] = acc

    @pl.when(m == 0)
    def _():
        ps_ref[...] = jnp.dot(xs_ref[...], w_ref[...], preferred_element_type=F32)

    @pl.when((m == 0) & (n == 0))
    def _():
        gs_ref[...] = jnp.dot(xs_ref[...], wg_ref[...], preferred_element_type=F32)


def _inproj_call(xn, w_main, w_gate, xsn, n_real, tile):
    lp, d = xn.shape
    s = xsn.shape[0]
    nm = lp // tile

    def held(n, m, lo, hi):
        return jnp.where(n < lo, 0, jnp.where(n > hi, nm - 1, m))

    def own(lo, hi):
        return lambda n, m: (held(n, m, lo, hi), 0)

    def own_t(lo, hi):
        return lambda n, m: (0, held(n, m, lo, hi))

    return pl.pallas_call(
        _inproj_kernel,
        grid=(N_COLB, nm),
        in_specs=[
            pl.BlockSpec((tile, d), lambda n, m: (m, 0)),
            pl.BlockSpec((d, COLB), lambda n, m: (0, n)),
            pl.BlockSpec((d, LANES), lambda n, m: (0, 0)),
            pl.BlockSpec((s, d), lambda n, m: (0, 0)),
        ],
        out_specs=[
            pl.BlockSpec((COLB, tile), own_t(0, 0)),
            pl.BlockSpec((tile, COLB), own(1, 1)),
            pl.BlockSpec((tile, COLB), own(1, 1)),
            pl.BlockSpec((tile, COLB), own(2, 2)),
            pl.BlockSpec((COLB, tile), own_t(2, 2)),
            pl.BlockSpec((tile, COLB), lambda n, m: (jnp.where(n < 3, 0, m), jnp.maximum(n - 3, 0))),
            pl.BlockSpec((tile, LANES), own(0, 0)),
            pl.BlockSpec((s, COLB), lambda n, m: (0, n)),
            pl.BlockSpec((s, LANES), lambda n, m: (0, 0)),
        ],
        out_shape=[
            jax.ShapeDtypeStruct((COLB, lp), BF16),
            jax.ShapeDtypeStruct((n_real, COLB), F32),
            jax.ShapeDtypeStruct((lp, COLB), BF16),
            jax.ShapeDtypeStruct((n_real, COLB), F32),
            jax.ShapeDtypeStruct((COLB, lp), BF16),
            jax.ShapeDtypeStruct((lp, (N_COLB - 3) * COLB), F32),
            jax.ShapeDtypeStruct((lp, LANES), F32),
            jax.ShapeDtypeStruct((s, N_COLB * COLB), F32),
            jax.ShapeDtypeStruct((s, LANES), F32),
        ],
        compiler_params=_cparams(("arbitrary", "arbitrary")),
        name="in_proj",
    )(xn, w_main, w_gate, xsn)


def _gates_kernel(g_ref, b_ref, ltri_ref, gc_ref, ct_ref, lf_ref, carry_ref, *, n_real):
    i = pl.program_id(0)

    @pl.when(i == 0)
    def _():
        carry_ref[...] = jnp.zeros_like(carry_ref)

    x = g_ref[...] + b_ref[...]
    lane = lax.broadcasted_iota(jnp.int32, (CHUNK, LANES), 1)
    row = i * CHUNK + lax.broadcasted_iota(jnp.int32, (CHUNK, LANES), 0)
    valid = row < n_real
    ls = _log_sigmoid(x)
    is_mf = (lane >= G_MF) & (lane < G_MF + M_HEADS)
    summed = jnp.where(valid & ((lane < F_HEADS) | is_mf), ls, 0.0)
    cs = _dot3_l(ltri_ref[...], summed)
    glob = cs + carry_ref[...]
    out = jnp.where(lane < F_HEADS, glob, jnp.where(is_mf, cs, jnp.where(valid, x, NEG)))
    gc_ref[...] = out
    ct_ref[...] = out.T[0:16, :]
    lf_ref[...] = ls[:, 0:F_HEADS]
    carry_ref[...] = glob[CHUNK - 1:CHUNK, :]


def _gates_call(gates, bias_row, n_real):
    lp = gates.shape[0]
    nb = lp // CHUNK
    ltri = jnp.asarray(np.tril(np.ones((CHUNK, CHUNK), np.float32)), BF16)
    return pl.pallas_call(
        functools.partial(_gates_kernel, n_real=n_real),
        grid=(nb,),
        in_specs=[
            pl.BlockSpec((CHUNK, LANES), lambda i: (i, 0)),
            pl.BlockSpec((1, LANES), lambda i: (0, 0)),
            pl.BlockSpec((CHUNK, CHUNK), lambda i: (0, 0)),
        ],
        out_specs=[
            pl.BlockSpec((CHUNK, LANES), lambda i: (i, 0)),
            pl.BlockSpec((16, CHUNK), lambda i: (0, i)),
            pl.BlockSpec((CHUNK, F_HEADS), lambda i: (i, 0)),
        ],
        out_shape=[
            jax.ShapeDtypeStruct((lp, LANES), F32),
            jax.ShapeDtypeStruct((16, lp), F32),
            jax.ShapeDtypeStruct((n_real, F_HEADS), F32),
        ],
        scratch_shapes=[pltpu.VMEM((1, LANES), F32)],
        compiler_params=_cparams(("arbitrary",)),
        name="gates",
    )(gates, bias_row, ltri)


def _fox_kernel(qi_ref, ki_ref, qt_ref, k_ref, vt_ref, gck_ref, ctq_ref, fz_ref, o_ref,
                acc_ref, m_ref, l_ref, t_ref):
    step_id = pl.program_id(0)
    qi = qi_ref[step_id]
    ki = ki_ref[step_id]
    tk = k_ref.shape[0]
    tq = qt_ref.shape[1]
    cols = [(c0, LANES) for c0 in range(0, tq, LANES)]

    @pl.when(ki == 0)
    def _():
        m_ref[...] = jnp.full_like(m_ref, -jnp.inf)
        l_ref[...] = jnp.zeros_like(l_ref)
        acc_ref[...] = jnp.zeros_like(acc_ref)

    def step(diag):
        for h in range(F_HEADS):
            hs = slice(h * F_HDIM, (h + 1) * F_HDIM)
            ck = gck_ref[:, G_FOX + h:G_FOX + h + 1] * LOG2E
            for c0, cw in cols:
                cs = slice(c0, c0 + cw)
                nk = c0 + cw if diag else tk
                s = jnp.dot(k_ref[0:nk, hs], qt_ref[hs, cs], preferred_element_type=F32)
                t = s - ck[0:nk, :]
                if diag:
                    keep = (lax.broadcasted_iota(jnp.int32, (nk, cw), 0)
                            <= c0 + lax.broadcasted_iota(jnp.int32, (nk, cw), 1))
                    t = jnp.where(keep, t, -jnp.inf)
                t_ref[0:nk, 0:cw] = t
                cq = ctq_ref[G_FOX + h:G_FOX + h + 1, cs] * LOG2E
                m_prev = m_ref[h, :, cs]
                m_new = jnp.maximum(m_prev, jnp.max(t, axis=0, keepdims=True) + cq)
                p = jnp.exp2(t_ref[0:nk, 0:cw] - (m_new - cq))
                alpha = jnp.exp2(m_prev - m_new)
                l_ref[h, :, cs] = alpha * l_ref[h, :, cs] + jnp.sum(p, axis=0, keepdims=True)
                acc_ref[hs, cs] = alpha * acc_ref[hs, cs] + jnp.dot(
                    vt_ref[hs, 0:nk], p.astype(BF16), preferred_element_type=F32)
                m_ref[h, :, cs] = m_new

    @pl.when(ki < qi)
    def _():
        step(False)

    @pl.when(ki == qi)
    def _():
        step(True)
        for h in range(F_HEADS):
            hs = slice(h * F_HDIM, (h + 1) * F_HDIM)
            for c in range(tq // LANES):
                cs = slice(c * LANES, (c + 1) * LANES)
                o = (acc_ref[hs, cs] / l_ref[h, :, cs]).T
                o_ref[cs, hs] = (_silu(fz_ref[cs, hs]) * o).astype(BF16)


def _fox_call(qt, k, vt, ct, gc, p2, tile):
    lp = k.shape[0]
    nq = lp // tile
    pairs = [(q, kk) for q in range(nq) for kk in range(q + 1)]
    qi_list = jnp.asarray(np.array([p[0] for p in pairs], np.int32))
    ki_list = jnp.asarray(np.array([p[1] for p in pairs], np.int32))
    grid_spec = pltpu.PrefetchScalarGridSpec(
        num_scalar_prefetch=2,
        grid=(len(pairs),),
        in_specs=[
            pl.BlockSpec((F_WIDTH, tile), lambda s, qi, ki: (0, qi[s])),
            pl.BlockSpec((tile, F_WIDTH), lambda s, qi, ki: (ki[s], 0)),
            pl.BlockSpec((F_WIDTH, tile), lambda s, qi, ki: (0, ki[s])),
            pl.BlockSpec((tile, LANES), lambda s, qi, ki: (ki[s], 0)),
            pl.BlockSpec((16, tile), lambda s, qi, ki: (0, qi[s])),
            pl.BlockSpec((tile, COLB), lambda s, qi, ki: (qi[s], 0)),
        ],
        out_specs=pl.BlockSpec((tile, F_WIDTH), lambda s, qi, ki: (qi[s], 0)),
        scratch_shapes=[
            pltpu.VMEM((F_WIDTH, tile), F32),
            pltpu.VMEM((F_HEADS, 1, tile), F32),
            pltpu.VMEM((F_HEADS, 1, tile), F32),
            pltpu.VMEM((tile, LANES), F32),
        ],
    )
    return pl.pallas_call(
        _fox_kernel,
        grid_spec=grid_spec,
        out_shape=jax.ShapeDtypeStruct((lp, F_WIDTH), BF16),
        compiler_params=_cparams(("arbitrary",)),
        name="fox",
    )(qi_list, ki_list, qt, k, vt, gc, ct, p2)


def _mlstm_kernel(qk_ref, v_ref, o_ref, z_ref, gc_ref, ct_ref, gain_ref,
                  mp_ref, c_out, n_out, m_out, c_s, n_s, m_s):
    i = pl.program_id(0)

    @pl.when(i == 0)
    def _():
        c_s[...] = jnp.zeros_like(c_s)
        n_s[...] = jnp.zeros_like(n_s)
        m_s[...] = jnp.zeros_like(m_s)

    causal = (lax.broadcasted_iota(jnp.int32, (CHUNK, CHUNK), 0)
              >= lax.broadcasted_iota(jnp.int32, (CHUNK, CHUNK), 1))
    for h in range(M_HEADS):
        qs = slice(h * M_QKDIM, (h + 1) * M_QKDIM)
        ks = slice(M_QKW + h * M_QKDIM, M_QKW + (h + 1) * M_QKDIM)
        vs = slice(h * M_VDIM, (h + 1) * M_VDIM)
        q = qk_ref[:, qs]
        k = qk_ref[:, ks] * (M_QKDIM ** -0.5)
        v = v_ref[:, vs]
        bt_c = gc_ref[:, G_MF + h:G_MF + h + 1]
        it_c = gc_ref[:, G_MI + h:G_MI + h + 1]
        bt_r = ct_ref[G_MF + h:G_MF + h + 1, :]
        it_r = ct_ref[G_MI + h:G_MI + h + 1, :]
        m_prev = m_s[h][:, 0:1]
        dlog = jnp.where(causal, bt_c - bt_r + it_r, -jnp.inf)
        inter = bt_c + m_prev
        m_t = jnp.maximum(inter, jnp.max(dlog, axis=1, keepdims=True))
        dw = jnp.exp(dlog - m_t)
        iw = jnp.exp(inter - m_t)
        qb = q.astype(BF16)
        kb = k.astype(BF16)
        s = lax.dot_general(qb, kb, _NT, preferred_element_type=F32) * dw
        c_old = c_s[h]
        n_old = n_s[h]
        num = (jnp.dot(s.astype(BF16), v.astype(BF16), preferred_element_type=F32)
               + iw * lax.dot_general(qb, c_old.astype(BF16), _NT, preferred_element_type=F32))
        den = jnp.sum(s, axis=1, keepdims=True) + iw * jnp.sum(q * n_old, axis=1, keepdims=True)
        hh = num / jnp.maximum(jnp.abs(den), jnp.exp(-m_t))
        m_new = m_t[CHUNK - 1:CHUNK, :]
        bt_last = bt_c[CHUNK - 1:CHUNK, :]
        wl_c = jnp.exp(bt_last - bt_c + it_c - m_new)
        decay = jnp.exp(bt_last + m_prev - m_new)
        vw = (v * wl_c).astype(BF16)
        c_s[h] = decay * c_old + lax.dot_general(vw, kb, _TN, preferred_element_type=F32)
        n_s[h] = decay * n_old + jnp.sum(wl_c * k, axis=0, keepdims=True)
        m_s[h] = jnp.broadcast_to(m_new, (1, LANES))
        hn = hh * lax.rsqrt(jnp.mean(hh * hh, axis=1, keepdims=True) + EPS) * gain_ref[:, vs]
        mp_ref[:, vs] = (_silu(z_ref[:, vs]) * _sigmoid(o_ref[:, vs]) * hn).astype(BF16)

    @pl.when(i == pl.num_programs(0) - 1)
    def _():
        c_out[...] = c_s[...]
        n_out[...] = n_s[...]
        m_out[...] = m_s[...]


def _mlstm_call(p2, gc, ct, gain):
    lp = p2.shape[0]
    nb = lp // CHUNK
    const3 = lambda i: (0, 0, 0)
    return pl.pallas_call(
        _mlstm_kernel,
        grid=(nb,),
        in_specs=[
            pl.BlockSpec((CHUNK, COLB), lambda i: (i, 1)),
            pl.BlockSpec((CHUNK, COLB), lambda i: (i, 2)),
            pl.BlockSpec((CHUNK, COLB), lambda i: (i, 3)),
            pl.BlockSpec((CHUNK, COLB), lambda i: (i, 4)),
            pl.BlockSpec((CHUNK, LANES), lambda i: (i, 0)),
            pl.BlockSpec((16, CHUNK), lambda i: (0, i)),
            pl.BlockSpec((1, M_WIDTH), lambda i: (0, 0)),
        ],
        out_specs=[
            pl.BlockSpec((CHUNK, M_WIDTH), lambda i: (i, 0)),
            pl.BlockSpec((M_HEADS, M_VDIM, M_QKDIM), const3),
            pl.BlockSpec((M_HEADS, 1, M_QKDIM), const3),
            pl.BlockSpec((M_HEADS, 1, LANES), const3),
        ],
        out_shape=[
            jax.ShapeDtypeStruct((lp, M_WIDTH), BF16),
            jax.ShapeDtypeStruct((M_HEADS, M_VDIM, M_QKDIM), F32),
            jax.ShapeDtypeStruct((M_HEADS, 1, M_QKDIM), F32),
            jax.ShapeDtypeStruct((M_HEADS, 1, LANES), F32),
        ],
        scratch_shapes=[
            pltpu.VMEM((M_HEADS, M_VDIM, M_QKDIM), F32),
            pltpu.VMEM((M_HEADS, 1, M_QKDIM), F32),
            pltpu.VMEM((M_HEADS, 1, LANES), F32),
        ],
        compiler_params=_cparams(("arbitrary",)),
        name="mlstm",
    )(p2, p2, p2, p2, gc, ct, gain)


def _out_kernel(x_ref, fm_ref, fn_ref, mm_ref, mn_ref, wt_ref, wb_ref, g_ref, y_ref):
    t = x_ref.shape[0]
    cat_f = jnp.concatenate([fm_ref[N_META:t, :], fn_ref[...]], axis=0)
    cat_m = jnp.concatenate([mm_ref[N_META:t, :], mn_ref[...]], axis=0)
    y = (x_ref[...]
         + jnp.dot(cat_f, wt_ref[...], preferred_element_type=F32)
         + jnp.dot(cat_m, wb_ref[...], preferred_element_type=F32))
    y_ref[...] = y * lax.rsqrt(jnp.mean(y * y, axis=-1, keepdims=True) + EPS) * g_ref[...]


def _out_call(x, fpart, mpart, w_top, w_bot, gain, tile):
    seq, d = x.shape
    per = tile // N_META
    main = lambda i: (i, 0)
    nxt = lambda i: ((i + 1) * per, 0)
    const = lambda i: (0, 0)
    return pl.pallas_call(
        _out_kernel,
        grid=(seq // tile,),
        in_specs=[
            pl.BlockSpec((tile, d), main),
            pl.BlockSpec((tile, F_WIDTH), main),
            pl.BlockSpec((N_META, F_WIDTH), nxt),
            pl.BlockSpec((tile, M_WIDTH), main),
            pl.BlockSpec((N_META, M_WIDTH), nxt),
            pl.BlockSpec((F_WIDTH, d), const),
            pl.BlockSpec((M_WIDTH, d), const),
            pl.BlockSpec((1, d), const),
        ],
        out_specs=pl.BlockSpec((tile, d), main),
        out_shape=jax.ShapeDtypeStruct((seq, d), F32),
        compiler_params=_cparams(("arbitrary",)),
        name="out_proj",
    )(x, fpart, fpart, mpart, mpart, w_top, w_bot, gain)


def _out_s_kernel(x_ref, f_ref, m_ref, wt_ref, wb_ref, g_ref, y_ref):
    y = (x_ref[...]
         + jnp.dot(f_ref[...].astype(BF16), wt_ref[...], preferred_element_type=F32)
         + jnp.dot(m_ref[...].astype(BF16), wb_ref[...], preferred_element_type=F32))
    y_ref[...] = y * lax.rsqrt(jnp.mean(y * y, axis=-1, keepdims=True) + EPS) * g_ref[...]


def _out_s_call(xs, fps, mps, w_top, w_bot, gain):
    s, d = xs.shape
    return pl.pallas_call(
        _out_s_kernel,
        out_shape=jax.ShapeDtypeStruct((s, d), F32),
        compiler_params=pltpu.CompilerParams(vmem_limit_bytes=VMEM_LIMIT),
        name="out_proj_s",
    )(xs, fps, mps, w_top, w_bot, gain)


def _flat_consts(page, group):
    lane = np.arange(LANES)
    b, h = lane // F_HEADS, lane % F_HEADS
    same = h[:, None] == h[None, :]
    m_blk = (same & (b[:, None] > b[None, :])).astype(np.float32)
    t_blk = same.astype(np.float32)
    r = np.arange(group * (page * F_HEADS // LANES))
    u_row = r[None, :] > r[:, None]
    return (jnp.asarray(m_blk, BF16), jnp.asarray(t_blk, BF16), jnp.asarray(u_row.astype(np.float32), BF16))


def _paged_kernel(pt_ref, q_ref, kn_ref, vn_ref, gs_ref, b_ref, mb_ref, tb_ref, u_ref, ck_hbm, cv_hbm,
                  *rest, group, scale):
    lf_refs = rest[:group]
    o_ref, m_ref, l_ref, acc_ref, base_ref, kbuf, vbuf, sem = rest[group:]
    ng = pl.num_programs(1)
    total = pl.num_programs(0) * ng
    step = pl.program_id(0) * ng + pl.program_id(1)

    def page_copies(slot, pages):
        for j, page in enumerate(pages):
            yield pltpu.make_async_copy(ck_hbm.at[0, page], kbuf.at[slot, j], sem.at[0, slot])
            yield pltpu.make_async_copy(cv_hbm.at[0, page], vbuf.at[slot, j], sem.at[1, slot])

    def fetch(t):
        bb = t // ng
        first = (ng - 1 - t % ng) * group
        for c in page_copies(t % KV_SLOTS, [pt_ref[bb, first + j] for j in range(group)]):
            c.start()

    @pl.when(step == 0)
    def _():
        for t in range(KV_SLOTS - 1):
            pl.when(t < total)(functools.partial(fetch, t))

    @pl.when(step + (KV_SLOTS - 1) < total)
    def _():
        fetch(step + (KV_SLOTS - 1))

    slot = step % KV_SLOTS
    for c in page_copies(slot, [0] * group):
        c.wait()

    g = pl.program_id(1)
    nblk = lf_refs[0].shape[1]
    w = nblk * LANES
    own = (lax.broadcasted_iota(jnp.int32, (F_HEADS, w), 1) % F_HEADS
           == lax.broadcasted_iota(jnp.int32, (F_HEADS, w), 0))

    @pl.when(g == 0)
    def _():
        m_ref[...] = jnp.sum(q_ref[0] * kn_ref[0], axis=1, keepdims=True) * scale
        l_ref[...] = jnp.ones_like(l_ref)
        acc_ref[...] = vn_ref[0]
        cn = _log_sigmoid(gs_ref[0] + b_ref[...])
        head_lane = lax.broadcasted_iota(jnp.int32, (1, LANES), 1) < F_HEADS
        base_ref[...] = _dot3_r(jnp.where(head_lane, cn, 0.0), tb_ref[...])

    xs = jnp.concatenate([r[0] for r in lf_refs], axis=0)
    tot = _dot3_r(xs, tb_ref[...])
    bias = _dot3_r(xs, mb_ref[...]) + _dot3_l(u_ref[...], tot) + base_ref[...]
    base_ref[...] = base_ref[...] + jnp.sum(tot, axis=0, keepdims=True)

    qb = q_ref[0].astype(BF16)
    s = []
    for j in range(group):
        kb = kbuf[slot, j].reshape(w, F_HDIM).astype(BF16)
        bj = jnp.concatenate([bias[j * nblk + k:j * nblk + k + 1, :] for k in range(nblk)], axis=1)
        sj = lax.dot_general(qb, kb, _NT, preferred_element_type=F32) * scale + bj
        s.append(jnp.where(own, sj, -jnp.inf))
    s = jnp.concatenate(s, axis=1)
    m_prev = m_ref[...]
    m_new = jnp.maximum(m_prev, jnp.max(s, axis=1, keepdims=True))
    alpha = jnp.exp(m_prev - m_new)
    p = jnp.exp(s - m_new)
    l_ref[...] = alpha * l_ref[...] + jnp.sum(p, axis=1, keepdims=True)
    pv = None
    for j in range(group):
        pj = p[:, j * w:(j + 1) * w].astype(BF16)
        vb = vbuf[slot, j].reshape(w, F_HDIM).astype(BF16)
        d = jnp.dot(pj, vb, preferred_element_type=F32)
        pv = d if pv is None else pv + d
    acc_ref[...] = alpha * acc_ref[...] + pv
    m_ref[...] = m_new

    @pl.when(g == pl.num_programs(1) - 1)
    def _():
        o_ref[0] = acc_ref[...] / l_ref[...]


def _paged_call(page_table, q8, kn8, vn8, gs3, bias_row, cache_k, cache_v, lf_rows, group):
    b, n_pages = page_table.shape
    page = cache_k.shape[2]
    w = page * F_HEADS
    ng = n_pages // group
    m_blk, t_blk, u_row = _flat_consts(page, group)
    nrow = u_row.shape[0]

    def page_map3(j):
        return lambda bi, g, pt: (pt[bi, (ng - 1 - g) * group + j], 0, 0)

    head = pl.BlockSpec((1, F_HEADS, F_HDIM), lambda bi, g, pt: (bi, 0, 0))
    const = lambda bi, g, pt: (0, 0)
    grid_spec = pltpu.PrefetchScalarGridSpec(
        num_scalar_prefetch=1,
        grid=(b, ng),
        in_specs=(
            [head, head, head,
             pl.BlockSpec((1, 1, LANES), lambda bi, g, pt: (bi, 0, 0)),
             pl.BlockSpec((1, LANES), const),
             pl.BlockSpec((LANES, LANES), const),
             pl.BlockSpec((LANES, LANES), const),
             pl.BlockSpec((nrow, nrow), const),
             pl.BlockSpec(memory_space=pl.ANY),
             pl.BlockSpec(memory_space=pl.ANY)]
            + [pl.BlockSpec((1, w // LANES, LANES), page_map3(j)) for j in range(group)]
        ),
        out_specs=pl.BlockSpec((1, F_HEADS, F_HDIM), lambda bi, g, pt: (bi, 0, 0)),
        scratch_shapes=[
            pltpu.VMEM((F_HEADS, 1), F32),
            pltpu.VMEM((F_HEADS, 1), F32),
            pltpu.VMEM((F_HEADS, F_HDIM), F32),
            pltpu.VMEM((1, LANES), F32),
            pltpu.VMEM((KV_SLOTS, group, page, F_HEADS, F_HDIM), F32),
            pltpu.VMEM((KV_SLOTS, group, page, F_HEADS, F_HDIM), F32),
            pltpu.SemaphoreType.DMA((2, KV_SLOTS)),
        ],
    )
    return pl.pallas_call(
        functools.partial(_paged_kernel, group=group, scale=F_HDIM ** -0.5),
        grid_spec=grid_spec,
        out_shape=jax.ShapeDtypeStruct((b, F_HEADS, F_HDIM), F32),
        compiler_params=_cparams(("arbitrary", "arbitrary")),
        name="paged_attn",
    )(page_table, q8, kn8, vn8, gs3, bias_row, m_blk, t_blk, u_row,
      cache_k, cache_v, *([lf_rows] * group))


def _step_kernel(qk_ref, v_ref, o_ref, z_ref, fz_ref, gs_ref, b_ref, fo_ref, gain_ref,
                 c_ref, n_ref, m_ref,
                 c_out, n_out, m_out, lf_out, fp_out, mp_out, *, sb):
    g = gs_ref[...] + b_ref[...]
    ls = _log_sigmoid(g)
    lf_out[...] = ls[:, G_FOX:G_FOX + F_HEADS]
    fp_out[...] = _silu(fz_ref[...]) * fo_ref[...]
    first_row = lax.broadcasted_iota(jnp.int32, (CHUNK, 1), 0) == 0
    for b in range(sb):
        for h in range(M_HEADS):
            qs = slice(h * M_QKDIM, (h + 1) * M_QKDIM)
            ks = slice(M_QKW + h * M_QKDIM, M_QKW + (h + 1) * M_QKDIM)
            vs = slice(h * M_VDIM, (h + 1) * M_VDIM)
            q = qk_ref[b:b + 1, qs]
            k = qk_ref[b:b + 1, ks] * (M_QKDIM ** -0.5)
            v = v_ref[b:b + 1, vs]
            logi = g[b:b + 1, G_MI + h:G_MI + h + 1]
            logf = ls[b:b + 1, G_MF + h:G_MF + h + 1]
            m_prev = m_ref[b:b + 1, h:h + 1]
            inter = logf + m_prev
            m_t = jnp.maximum(inter, logi)
            dw = jnp.exp(logi - m_t)
            iw = jnp.exp(inter - m_t)
            c_old = c_ref[b, h]
            n_old = n_ref[b, h:h + 1, :]
            s = jnp.sum(q * k, axis=1, keepdims=True) * dw
            cq = lax.dot_general(q.astype(BF16), c_old.astype(BF16), _NT, preferred_element_type=F32)
            num = s * v + iw * cq
            den = s + iw * jnp.sum(q * n_old, axis=1, keepdims=True)
            hh = num / jnp.maximum(jnp.abs(den), jnp.exp(-m_t))
            v_pad = jnp.where(first_row, jnp.broadcast_to(v * dw, (CHUNK, M_VDIM)), 0.0).astype(BF16)
            k_pad = jnp.broadcast_to(k, (CHUNK, M_QKDIM)).astype(BF16)
            c_out[b, h] = iw * c_old + lax.dot_general(v_pad, k_pad, _TN, preferred_element_type=F32)
            n_out[b, h:h + 1, :] = iw * n_old + dw * k
            m_out[b:b + 1, h:h + 1] = m_t
            hn = hh * lax.rsqrt(jnp.mean(hh * hh, axis=1, keepdims=True) + EPS) * gain_ref[:, vs]
            mp_out[b:b + 1, vs] = _silu(z_ref[b:b + 1, vs]) * _sigmoid(o_ref[b:b + 1, vs]) * hn


def _step_call(ps, gs, bias_row, fo, gain, state_c, state_n, state_m, sb):
    b = ps.shape[0]
    col = lambda c: pl.BlockSpec((sb, COLB), lambda i: (i, c))
    const = lambda i: (0, 0)
    return pl.pallas_call(
        functools.partial(_step_kernel, sb=sb),
        grid=(b // sb,),
        in_specs=[
            col(4), col(5), col(6), col(7), col(3),
            pl.BlockSpec((sb, LANES), lambda i: (i, 0)),
            pl.BlockSpec((1, LANES), const),
            pl.BlockSpec((sb, F_WIDTH), lambda i: (i, 0)),
            pl.BlockSpec((1, M_WIDTH), const),
            pl.BlockSpec((sb, M_HEADS, M_VDIM, M_QKDIM), lambda i: (i, 0, 0, 0)),
            pl.BlockSpec((sb, M_HEADS, M_QKDIM), lambda i: (i, 0, 0)),
            pl.BlockSpec((sb, M_HEADS), lambda i: (i, 0)),
        ],
        out_specs=[
            pl.BlockSpec((sb, M_HEADS, M_VDIM, M_QKDIM), lambda i: (i, 0, 0, 0)),
            pl.BlockSpec((sb, M_HEADS, M_QKDIM), lambda i: (i, 0, 0)),
            pl.BlockSpec((sb, M_HEADS), lambda i: (i, 0)),
            pl.BlockSpec((sb, F_HEADS), lambda i: (i, 0)),
            pl.BlockSpec((sb, F_WIDTH), lambda i: (i, 0)),
            pl.BlockSpec((sb, M_WIDTH), lambda i: (i, 0)),
        ],
        out_shape=[
            jax.ShapeDtypeStruct(state_c.shape, F32),
            jax.ShapeDtypeStruct(state_n.shape, F32),
            jax.ShapeDtypeStruct(state_m.shape, F32),
            jax.ShapeDtypeStruct((b, F_HEADS), F32),
            jax.ShapeDtypeStruct((b, F_WIDTH), F32),
            jax.ShapeDtypeStruct((b, M_WIDTH), F32),
        ],
        compiler_params=_cparams(("arbitrary",)),
        name="mlstm_step",
    )(ps, ps, ps, ps, ps, gs, bias_row, fo, gain, state_c, state_n, state_m)


_COL_START = np.concatenate([[0], np.cumsum(PROJ_SIZES)])


def _repack_kernel(wt_ref, wp_ref, wg_ref):
    o = _COL_START
    for n, a in enumerate((o[0], o[1], o[2], o[4], o[5], o[7], o[10], o[11])):
        wp_ref[:, n * COLB:(n + 1) * COLB] = wt_ref[a:a + COLB, :].T.astype(BF16)
    assert o[3] % 8 == 0 and o[8] % 8 == 0 and o[9] == o[8] + M_HEADS and o[10] == o[9] + M_HEADS
    assert (G_FOX, G_MI, G_MF) == (0, F_HEADS, F_HEADS + M_HEADS)
    pad = jnp.zeros((LANES - F_HEADS - 2 * M_HEADS, wt_ref.shape[1]), F32)
    wg = jnp.concatenate([wt_ref[o[3]:o[3] + F_HEADS, :], wt_ref[o[8]:o[8] + 2 * M_HEADS, :], pad], axis=0)
    wg_ref[...] = wg.T.astype(BF16)


def _repack_call(wt):
    dp, d = wt.shape
    tc = _pick_tile(d, (256, 128))
    return pl.pallas_call(
        _repack_kernel,
        grid=(d // tc,),
        in_specs=[pl.BlockSpec((dp, tc), lambda i: (0, i))],
        out_specs=[pl.BlockSpec((tc, N_COLB * COLB), lambda i: (i, 0)),
                   pl.BlockSpec((tc, LANES), lambda i: (i, 0))],
        out_shape=[jax.ShapeDtypeStruct((d, N_COLB * COLB), BF16),
                   jax.ShapeDtypeStruct((d, LANES), BF16)],
        compiler_params=_cparams(("arbitrary",)),
        name="repack",
    )(wt)


def kernel(x_prompt, x_sample, cache_k, cache_v, cache_logf, state_C, state_n, state_m,
           page_table, meta_tokens, norm_gain, w_in, b_fox_f, b_m_i, b_m_f, mh_gain,
           w_out, final_gain):
    batch, seq, d = x_prompt.shape
    depth = w_in.shape[0]
    assert batch == 1 and depth == 1 and x_sample.shape[1] == 1
    assert seq % CHUNK == 0
    s = x_sample.shape[0]
    n_real = N_META + seq
    lp = pl.cdiv(n_real, CHUNK) * CHUNK
    tile = _pick_tile(lp, (640, 512, 384, 256, 128))
    tile_out = _pick_tile(seq, (512, 256, 128))
    n_pool, page = cache_k.shape[1], cache_k.shape[2]
    n_pages = page_table.shape[1]
    assert page == CHUNK and n_pages % 8 == 0 and s % 8 == 0

    w_main, w_gate = _repack_call(w_in[0].T)
    w_o = w_out[0].astype(BF16)
    w_top, w_bot = w_o[:F_WIDTH], w_o[F_WIDTH:]
    bias_row = jnp.concatenate(
        [b_fox_f[0], b_m_i[0], b_m_f[0], jnp.zeros((LANES - F_HEADS - 2 * M_HEADS,), F32)])[None, :]
    gain = norm_gain[0][None, :]
    mh = mh_gain[0][None, :]
    fgain = final_gain[None, :]
    x2 = x_prompt[0]
    xs2 = x_sample[:, 0, :]

    xn, xsn = _norm_call(x2, meta_tokens, gain, xs2, lp, tile)
    qt_bf, k_out, k_bf, v_out, vt_bf, p2, gates, ps, gs = _inproj_call(xn, w_main, w_gate, xsn, n_real, tile)
    gc, ct, lf_p = _gates_call(gates, bias_row, n_real)

    fpart = _fox_call(qt_bf, k_bf, vt_bf, ct, gc, p2, tile)
    mpart, c_p, n_p, m_p = _mlstm_call(p2, gc, ct, mh)
    y_p = _out_call(x2, fpart, mpart, w_top, w_bot, fgain, tile_out)

    hd = (F_HEADS, F_HDIM)
    lf_rows = cache_logf.reshape(depth * n_pool, page * F_HEADS // LANES, LANES)
    fo = _paged_call(page_table,
                     ps[:, 0:COLB].reshape(s, *hd), ps[:, COLB:2 * COLB].reshape(s, *hd),
                     ps[:, 2 * COLB:3 * COLB].reshape(s, *hd), gs.reshape(s, 1, LANES), bias_row,
                     cache_k, cache_v, lf_rows, group=8)
    c_s, n_s, m_s, lf_s, fps, mps = _step_call(ps, gs, bias_row, fo.reshape(s, F_WIDTH), mh,
                                               state_C[0], state_n[0], state_m[0], sb=8)
    y_s = _out_s_call(xs2, fps, mps, w_top, w_bot, fgain)

    return (
        y_p[None],
        y_s[:, None, :],
        k_out.reshape(1, 1, n_real, *hd),
        v_out.reshape(1, 1, n_real, *hd),
        lf_p.reshape(1, 1, n_real, F_HEADS),
        c_p[None, None],
        n_p[:, 0, :][None, None],
        m_p[:, 0, 0][None, None],
        ps[:, COLB:2 * COLB].reshape(1, s, 1, *hd),
        ps[:, 2 * COLB:3 * COLB].reshape(1, s, 1, *hd),
        lf_s.reshape(1, s, 1, F_HEADS),
        c_s[None],
        n_s[None],
        m_s[None],
    )
```

```python
import functools

import numpy as np
import jax
import jax.numpy as jnp
from jax import lax
from jax.experimental import pallas as pl
from jax.experimental.pallas import tpu as pltpu

F32 = jnp.float32
BF16 = jnp.bfloat16

N_META = 16
EPS = 1e-6
F_HEADS = 8
F_HDIM = 128
F_WIDTH = F_HEADS * F_HDIM
M_HEADS = 4
M_VDIM = 256
M_QKDIM = 128
M_WIDTH = M_HEADS * M_VDIM
M_QKW = M_HEADS * M_QKDIM
PROJ_SIZES = (F_WIDTH, F_WIDTH, F_WIDTH, F_HEADS, F_WIDTH,
              M_QKW, M_QKW, M_WIDTH, M_HEADS, M_HEADS, M_WIDTH, M_WIDTH)

LANES = 128
CHUNK = 128
COLB = 1024
MXU_N = 256
N_COLB = 8
G_FOX = 0
G_MI = 8
G_MF = 12
NEG = -1e30
LOG2E = 1.4426950408889634
QSCALE = F_HDIM ** -0.5 * LOG2E
VMEM_LIMIT = 56 * 1024 * 1024
KV_SLOTS = 3

_NT = (((1,), (1,)), ((), ()))
_TN = (((0,), (0,)), ((), ()))


def _cparams(sem, vmem=VMEM_LIMIT):
    return pltpu.CompilerParams(dimension_semantics=sem, vmem_limit_bytes=vmem)


def _pick_tile(n, cands):
    for c in cands:
        if n % c == 0:
            return c
    raise ValueError(f"no tile for {n}")


def _log_sigmoid(x):
    return jnp.minimum(x, 0.0) - jnp.log1p(jnp.exp(-jnp.abs(x)))


def _sigmoid(x):
    return 1.0 / (1.0 + jnp.exp(-x))


def _silu(x):
    return x * _sigmoid(x)


def _split3(x):
    x1 = x.astype(BF16)
    r1 = x - x1.astype(F32)
    x2 = r1.astype(BF16)
    r2 = r1 - x2.astype(F32)
    x3 = r2.astype(BF16)
    return x1, x2, x3


def _dot3_l(a_bf, x):
    x1, x2, x3 = _split3(x)
    d = lambda v: jnp.dot(a_bf, v, preferred_element_type=F32)
    return (d(x3) + d(x2)) + d(x1)


def _dot3_r(x, b_bf):
    x1, x2, x3 = _split3(x)
    d = lambda v: jnp.dot(v, b_bf, preferred_element_type=F32)
    return (d(x3) + d(x2)) + d(x1)


def _norm_kernel(x_ref, prev_ref, meta_ref, g_ref, xs_ref, xn_ref, xsn_ref, *, tile, n_real):
    i = pl.program_id(0)
    g = g_ref[...]

    def nrm(x):
        return x * lax.rsqrt(jnp.mean(x * x, axis=-1, keepdims=True) + EPS) * g

    top = jnp.where(i == 0, meta_ref[...], prev_ref[...])
    xn_ref[0:N_META, :] = nrm(top).astype(BF16)
    body = x_ref[0:tile - N_META, :]
    row = i * tile + N_META + lax.broadcasted_iota(jnp.int32, (tile - N_META, 1), 0)
    xn_ref[N_META:tile, :] = jnp.where(row < n_real, nrm(body), 0.0).astype(BF16)

    @pl.when(i == 0)
    def _():
        xsn_ref[...] = nrm(xs_ref[...]).astype(BF16)


def _norm_call(x, meta, gain, xs, lp, tile):
    seq, d = x.shape
    n_real = seq + N_META
    nblk = pl.cdiv(seq, tile)
    per = tile // N_META
    nprev = seq // N_META
    s = xs.shape[0]
    return pl.pallas_call(
        functools.partial(_norm_kernel, tile=tile, n_real=n_real),
        grid=(lp // tile,),
        in_specs=[
            pl.BlockSpec((tile, d), lambda i: (jnp.minimum(i, nblk - 1), 0)),
            pl.BlockSpec((N_META, d), lambda i: (jnp.clip(i * per - 1, 0, nprev - 1), 0)),
            pl.BlockSpec((N_META, d), lambda i: (0, 0)),
            pl.BlockSpec((1, d), lambda i: (0, 0)),
            pl.BlockSpec((s, d), lambda i: (0, 0)),
        ],
        out_specs=[
            pl.BlockSpec((tile, d), lambda i: (i, 0)),
            pl.BlockSpec((s, d), lambda i: (0, 0)),
        ],
        out_shape=[jax.ShapeDtypeStruct((lp, d), BF16), jax.ShapeDtypeStruct((s, d), BF16)],
        compiler_params=_cparams(("arbitrary",)),
        name="norm",
    )(x, x, meta, gain, xs)


def _inproj_kernel(x_ref, w_ref, wg_ref, xs_ref,
                   q_ref, ko_ref, kb_ref, vo_ref, vb_ref, p2_ref, g_ref, ps_ref, gs_ref):
    n = pl.program_id(0)
    m = pl.program_id(1)

    def column_chunks():
        for c0 in range(0, COLB, MXU_N):
            cs = slice(c0, c0 + MXU_N)
            yield cs, jnp.dot(x_ref[...], w_ref[:, cs], preferred_element_type=F32)

    @pl.when(n == 0)
    def _():
        for cs, acc in column_chunks():
            q_ref[cs, :] = (acc * QSCALE).T.astype(BF16)
        g_ref[...] = jnp.dot(x_ref[...], wg_ref[...], preferred_element_type=F32)

    @pl.when(n == 1)
    def _():
        for cs, acc in column_chunks():
            ko_ref[:, cs] = acc
            kb_ref[:, cs] = acc.astype(BF16)

    @pl.when(n == 2)
    def _():
        for cs, acc in column_chunks():
            vo_ref[:, cs] = acc
            vb_ref[cs, :] = acc.T.astype(BF16)

    @pl.when(n >= 3)
    def _():
        for cs, acc in column_chunks():
            p2_ref[:, cs] = acc

    @pl.when(m == 0)
    def _():
        ps_ref[...] = jnp.dot(xs_ref[...], w_ref[...], preferred_element_type=F32)

    @pl.when((m == 0) & (n == 0))
    def _():
        gs_ref[...] = jnp.dot(xs_ref[...], wg_ref[...], preferred_element_type=F32)


def _inproj_call(xn, w_main, w_gate, xsn, n_real, tile):
    lp, d = xn.shape
    s = xsn.shape[0]
    nm = lp // tile

    def held(n, m, lo, hi):
        return jnp.where(n < lo, 0, jnp.where(n > hi, nm - 1, m))

    def own(lo, hi):
        return lambda n, m: (held(n, m, lo, hi), 0)

    def own_t(lo, hi):
        return lambda n, m: (0, held(n, m, lo, hi))

    return pl.pallas_call(
        _inproj_kernel,
        grid=(N_COLB, nm),
        in_specs=[
            pl.BlockSpec((tile, d), lambda n, m: (m, 0)),
            pl.BlockSpec((d, COLB), lambda n, m: (0, n)),
            pl.BlockSpec((d, LANES), lambda n, m: (0, 0)),
            pl.BlockSpec((s, d), lambda n, m: (0, 0)),
        ],
        out_specs=[
            pl.BlockSpec((COLB, tile), own_t(0, 0)),
            pl.BlockSpec((tile, COLB), own(1, 1)),
            pl.BlockSpec((tile, COLB), own(1, 1)),
            pl.BlockSpec((tile, COLB), own(2, 2)),
            pl.BlockSpec((COLB, tile), own_t(2, 2)),
            pl.BlockSpec((tile, COLB), lambda n, m: (jnp.where(n < 3, 0, m), jnp.maximum(n - 3, 0))),
            pl.BlockSpec((tile, LANES), own(0, 0)),
            pl.BlockSpec((s, COLB), lambda n, m: (0, n)),
            pl.BlockSpec((s, LANES), lambda n, m: (0, 0)),
        ],
        out_shape=[
            jax.ShapeDtypeStruct((COLB, lp), BF16),
            jax.ShapeDtypeStruct((n_real, COLB), F32),
            jax.ShapeDtypeStruct((lp, COLB), BF16),
            jax.ShapeDtypeStruct((n_real, COLB), F32),
            jax.ShapeDtypeStruct((COLB, lp), BF16),
            jax.ShapeDtypeStruct((lp, (N_COLB - 3) * COLB), F32),
            jax.ShapeDtypeStruct((lp, LANES), F32),
            jax.ShapeDtypeStruct((s, N_COLB * COLB), F32),
            jax.ShapeDtypeStruct((s, LANES), F32),
        ],
        compiler_params=_cparams(("arbitrary", "arbitrary")),
        name="in_proj",
    )(xn, w_main, w_gate, xsn)


def _gates_kernel(g_ref, b_ref, ltri_ref, gc_ref, ct_ref, lf_ref, carry_ref, *, n_real):
    i = pl.program_id(0)
    rows = g_ref.shape[0]

    @pl.when(i == 0)
    def _():
        carry_ref[...] = jnp.zeros_like(carry_ref)

    lane = lax.broadcasted_iota(jnp.int32, (CHUNK, LANES), 1)
    is_mf = (lane >= G_MF) & (lane < G_MF + M_HEADS)
    carry = carry_ref[...]
    for u in range(rows // CHUNK):
        rs = slice(u * CHUNK, (u + 1) * CHUNK)
        x = g_ref[rs, :] + b_ref[...]
        row = i * rows + u * CHUNK + lax.broadcasted_iota(jnp.int32, (CHUNK, LANES), 0)
        valid = row < n_real
        ls = _log_sigmoid(x)
        summed = jnp.where(valid & ((lane < F_HEADS) | is_mf), ls, 0.0)
        cs = _dot3_l(ltri_ref[...], summed)
        glob = cs + carry
        out = jnp.where(lane < F_HEADS, glob, jnp.where(is_mf, cs, jnp.where(valid, x, NEG)))
        gc_ref[rs, :] = out
        ct_ref[:, rs] = out.T[0:16, :]
        lf_ref[rs, :] = ls[:, 0:F_HEADS]
        carry = glob[CHUNK - 1:CHUNK, :]
    carry_ref[...] = carry


def _gates_call(gates, bias_row, n_real, tile):
    lp = gates.shape[0]
    ltri = jnp.asarray(np.tril(np.ones((CHUNK, CHUNK), np.float32)), BF16)
    return pl.pallas_call(
        functools.partial(_gates_kernel, n_real=n_real),
        grid=(lp // tile,),
        in_specs=[
            pl.BlockSpec((tile, LANES), lambda i: (i, 0)),
            pl.BlockSpec((1, LANES), lambda i: (0, 0)),
            pl.BlockSpec((CHUNK, CHUNK), lambda i: (0, 0)),
        ],
        out_specs=[
            pl.BlockSpec((tile, LANES), lambda i: (i, 0)),
            pl.BlockSpec((16, tile), lambda i: (0, i)),
            pl.BlockSpec((tile, F_HEADS), lambda i: (i, 0)),
        ],
        out_shape=[
            jax.ShapeDtypeStruct((lp, LANES), F32),
            jax.ShapeDtypeStruct((16, lp), F32),
            jax.ShapeDtypeStruct((n_real, F_HEADS), F32),
        ],
        scratch_shapes=[pltpu.VMEM((1, LANES), F32)],
        compiler_params=_cparams(("arbitrary",)),
        name="gates",
    )(gates, bias_row, ltri)


def _fox_kernel(qi_ref, ki_ref, qt_ref, k_ref, vt_ref, gck_ref, ctq_ref, fz_ref, o_ref,
                acc_ref, m_ref, l_ref, t_ref):
    step_id = pl.program_id(0)
    qi = qi_ref[step_id]
    ki = ki_ref[step_id]
    tk = k_ref.shape[0]
    tq = qt_ref.shape[1]
    cols = [(c0, LANES) for c0 in range(0, tq, LANES)]

    @pl.when(ki == 0)
    def _():
        m_ref[...] = jnp.full_like(m_ref, -jnp.inf)
        l_ref[...] = jnp.zeros_like(l_ref)
        acc_ref[...] = jnp.zeros_like(acc_ref)

    def step(diag):
        for h in range(F_HEADS):
            hs = slice(h * F_HDIM, (h + 1) * F_HDIM)
            ck = gck_ref[:, G_FOX + h:G_FOX + h + 1] * LOG2E
            for c0, cw in cols:
                cs = slice(c0, c0 + cw)
                nk = c0 + cw if diag else tk
                s = jnp.dot(k_ref[0:nk, hs], qt_ref[hs, cs], preferred_element_type=F32)
                t = s - ck[0:nk, :]
                if diag:
                    keep = (lax.broadcasted_iota(jnp.int32, (nk, cw), 0)
                            <= c0 + lax.broadcasted_iota(jnp.int32, (nk, cw), 1))
                    t = jnp.where(keep, t, -jnp.inf)
                t_ref[0:nk, 0:cw] = t
                cq = ctq_ref[G_FOX + h:G_FOX + h + 1, cs] * LOG2E
                m_prev = m_ref[h, :, cs]
                m_new = jnp.maximum(m_prev, jnp.max(t, axis=0, keepdims=True) + cq)
                p = jnp.exp2(t_ref[0:nk, 0:cw] - (m_new - cq))
                alpha = jnp.exp2(m_prev - m_new)
                l_ref[h, :, cs] = alpha * l_ref[h, :, cs] + jnp.sum(p, axis=0, keepdims=True)
                acc_ref[hs, cs] = alpha * acc_ref[hs, cs] + jnp.dot(
                    vt_ref[hs, 0:nk], p.astype(BF16), preferred_element_type=F32)
                m_ref[h, :, cs] = m_new

    @pl.when(ki < qi)
    def _():
        step(False)

    @pl.when(ki == qi)
    def _():
        step(True)
        for h in range(F_HEADS):
            hs = slice(h * F_HDIM, (h + 1) * F_HDIM)
            for c in range(tq // LANES):
                cs = slice(c * LANES, (c + 1) * LANES)
                o = (acc_ref[hs, cs] / l_ref[h, :, cs]).T
                o_ref[cs, hs] = (_silu(fz_ref[cs, hs]) * o).astype(BF16)


def _fox_call(qt, k, vt, ct, gc, p2, tile):
    lp = k.shape[0]
    nq = lp // tile
    pairs = [(q, kk) for q in range(nq) for kk in range(q + 1)]
    qi_list = jnp.asarray(np.array([p[0] for p in pairs], np.int32))
    ki_list = jnp.asarray(np.array([p[1] for p in pairs], np.int32))
    grid_spec = pltpu.PrefetchScalarGridSpec(
        num_scalar_prefetch=2,
        grid=(len(pairs),),
        in_specs=[
            pl.BlockSpec((F_WIDTH, tile), lambda s, qi, ki: (0, qi[s])),
            pl.BlockSpec((tile, F_WIDTH), lambda s, qi, ki: (ki[s], 0)),
            pl.BlockSpec((F_WIDTH, tile), lambda s, qi, ki: (0, ki[s])),
            pl.BlockSpec((tile, LANES), lambda s, qi, ki: (ki[s], 0)),
            pl.BlockSpec((16, tile), lambda s, qi, ki: (0, qi[s])),
            pl.BlockSpec((tile, COLB), lambda s, qi, ki: (qi[s], 0)),
        ],
        out_specs=pl.BlockSpec((tile, F_WIDTH), lambda s, qi, ki: (qi[s], 0)),
        scratch_shapes=[
            pltpu.VMEM((F_WIDTH, tile), F32),
            pltpu.VMEM((F_HEADS, 1, tile), F32),
            pltpu.VMEM((F_HEADS, 1, tile), F32),
            pltpu.VMEM((tile, LANES), F32),
        ],
    )
    return pl.pallas_call(
        _fox_kernel,
        grid_spec=grid_spec,
        out_shape=jax.ShapeDtypeStruct((lp, F_WIDTH), BF16),
        compiler_params=_cparams(("arbitrary",)),
        name="fox",
    )(qi_list, ki_list, qt, k, vt, gc, ct, p2)


def _mlstm_kernel(qk_ref, v_ref, o_ref, z_ref, gc_ref, ct_ref, gain_ref,
                  mp_ref, c_out, n_out, m_out, c_s, n_s, m_s):
    i = pl.program_id(0)

    @pl.when(i == 0)
    def _():
        c_s[...] = jnp.zeros_like(c_s)
        n_s[...] = jnp.zeros_like(n_s)
        m_s[...] = jnp.zeros_like(m_s)

    causal = (lax.broadcasted_iota(jnp.int32, (CHUNK, CHUNK), 0)
              >= lax.broadcasted_iota(jnp.int32, (CHUNK, CHUNK), 1))
    for h in range(M_HEADS):
        qs = slice(h * M_QKDIM, (h + 1) * M_QKDIM)
        ks = slice(M_QKW + h * M_QKDIM, M_QKW + (h + 1) * M_QKDIM)
        vs = slice(h * M_VDIM, (h + 1) * M_VDIM)
        q = qk_ref[:, qs]
        k = qk_ref[:, ks] * (M_QKDIM ** -0.5)
        v = v_ref[:, vs]
        bt_c = gc_ref[:, G_MF + h:G_MF + h + 1]
        it_c = gc_ref[:, G_MI + h:G_MI + h + 1]
        bt_r = ct_ref[G_MF + h:G_MF + h + 1, :]
        it_r = ct_ref[G_MI + h:G_MI + h + 1, :]
        m_prev = m_s[h][:, 0:1]
        dlog = jnp.where(causal, bt_c - bt_r + it_r, -jnp.inf)
        inter = bt_c + m_prev
        m_t = jnp.maximum(inter, jnp.max(dlog, axis=1, keepdims=True))
        dw = jnp.exp(dlog - m_t)
        iw = jnp.exp(inter - m_t)
        qb = q.astype(BF16)
        kb = k.astype(BF16)
        s = lax.dot_general(qb, kb, _NT, preferred_element_type=F32) * dw
        c_old = c_s[h]
        n_old = n_s[h]
        num = (jnp.dot(s.astype(BF16), v.astype(BF16), preferred_element_type=F32)
               + iw * lax.dot_general(qb, c_old.astype(BF16), _NT, preferred_element_type=F32))
        den = jnp.sum(s, axis=1, keepdims=True) + iw * jnp.sum(q * n_old, axis=1, keepdims=True)
        hh = num / jnp.maximum(jnp.abs(den), jnp.exp(-m_t))
        m_new = m_t[CHUNK - 1:CHUNK, :]
        bt_last = bt_c[CHUNK - 1:CHUNK, :]
        wl_c = jnp.exp(bt_last - bt_c + it_c - m_new)
        decay = jnp.exp(bt_last + m_prev - m_new)
        vw = (v * wl_c).astype(BF16)
        c_s[h] = decay * c_old + lax.dot_general(vw, kb, _TN, preferred_element_type=F32)
        n_s[h] = decay * n_old + jnp.sum(wl_c * k, axis=0, keepdims=True)
        m_s[h] = jnp.broadcast_to(m_new, (1, LANES))
        hn = hh * lax.rsqrt(jnp.mean(hh * hh, axis=1, keepdims=True) + EPS) * gain_ref[:, vs]
        mp_ref[:, vs] = (_silu(z_ref[:, vs]) * _sigmoid(o_ref[:, vs]) * hn).astype(BF16)

    @pl.when(i == pl.num_programs(0) - 1)
    def _():
        c_out[...] = c_s[...]
        n_out[...] = n_s[...]
        m_out[...] = m_s[...]


def _mlstm_call(p2, gc, ct, gain):
    lp = p2.shape[0]
    nb = lp // CHUNK
    const3 = lambda i: (0, 0, 0)
    return pl.pallas_call(
        _mlstm_kernel,
        grid=(nb,),
        in_specs=[
            pl.BlockSpec((CHUNK, COLB), lambda i: (i, 1)),
            pl.BlockSpec((CHUNK, COLB), lambda i: (i, 2)),
            pl.BlockSpec((CHUNK, COLB), lambda i: (i, 3)),
            pl.BlockSpec((CHUNK, COLB), lambda i: (i, 4)),
            pl.BlockSpec((CHUNK, LANES), lambda i: (i, 0)),
            pl.BlockSpec((16, CHUNK), lambda i: (0, i)),
            pl.BlockSpec((1, M_WIDTH), lambda i: (0, 0)),
        ],
        out_specs=[
            pl.BlockSpec((CHUNK, M_WIDTH), lambda i: (i, 0)),
            pl.BlockSpec((M_HEADS, M_VDIM, M_QKDIM), const3),
            pl.BlockSpec((M_HEADS, 1, M_QKDIM), const3),
            pl.BlockSpec((M_HEADS, 1, LANES), const3),
        ],
        out_shape=[
            jax.ShapeDtypeStruct((lp, M_WIDTH), BF16),
            jax.ShapeDtypeStruct((M_HEADS, M_VDIM, M_QKDIM), F32),
            jax.ShapeDtypeStruct((M_HEADS, 1, M_QKDIM), F32),
            jax.ShapeDtypeStruct((M_HEADS, 1, LANES), F32),
        ],
        scratch_shapes=[
            pltpu.VMEM((M_HEADS, M_VDIM, M_QKDIM), F32),
            pltpu.VMEM((M_HEADS, 1, M_QKDIM), F32),
            pltpu.VMEM((M_HEADS, 1, LANES), F32),
        ],
        compiler_params=_cparams(("arbitrary",)),
        name="mlstm",
    )(p2, p2, p2, p2, gc, ct, gain)


def _out_kernel(x_ref, fm_ref, fn_ref, mm_ref, mn_ref, wt_ref, wb_ref, g_ref, y_ref):
    t = x_ref.shape[0]
    cat_f = jnp.concatenate([fm_ref[N_META:t, :], fn_ref[...]], axis=0)
    cat_m = jnp.concatenate([mm_ref[N_META:t, :], mn_ref[...]], axis=0)
    y = (x_ref[...]
         + jnp.dot(cat_f, wt_ref[...], preferred_element_type=F32)
         + jnp.dot(cat_m, wb_ref[...], preferred_element_type=F32))
    y_ref[...] = y * lax.rsqrt(jnp.mean(y * y, axis=-1, keepdims=True) + EPS) * g_ref[...]


def _out_call(x, fpart, mpart, w_top, w_bot, gain, tile):
    seq, d = x.shape
    per = tile // N_META
    main = lambda i: (i, 0)
    nxt = lambda i: ((i + 1) * per, 0)
    const = lambda i: (0, 0)
    return pl.pallas_call(
        _out_kernel,
        grid=(seq // tile,),
        in_specs=[
            pl.BlockSpec((tile, d), main),
            pl.BlockSpec((tile, F_WIDTH), main),
            pl.BlockSpec((N_META, F_WIDTH), nxt),
            pl.BlockSpec((tile, M_WIDTH), main),
            pl.BlockSpec((N_META, M_WIDTH), nxt),
            pl.BlockSpec((F_WIDTH, d), const),
            pl.BlockSpec((M_WIDTH, d), const),
            pl.BlockSpec((1, d), const),
        ],
        out_specs=pl.BlockSpec((tile, d), main),
        out_shape=jax.ShapeDtypeStruct((seq, d), F32),
        compiler_params=_cparams(("arbitrary",)),
        name="out_proj",
    )(x, fpart, fpart, mpart, mpart, w_top, w_bot, gain)


def _out_s_kernel(x_ref, f_ref, m_ref, wt_ref, wb_ref, g_ref, y_ref):
    y = (x_ref[...]
         + jnp.dot(f_ref[...].astype(BF16), wt_ref[...], preferred_element_type=F32)
         + jnp.dot(m_ref[...].astype(BF16), wb_ref[...], preferred_element_type=F32))
    y_ref[...] = y * lax.rsqrt(jnp.mean(y * y, axis=-1, keepdims=True) + EPS) * g_ref[...]


def _out_s_call(xs, fps, mps, w_top, w_bot, gain):
    s, d = xs.shape
    return pl.pallas_call(
        _out_s_kernel,
        out_shape=jax.ShapeDtypeStruct((s, d), F32),
        compiler_params=pltpu.CompilerParams(vmem_limit_bytes=VMEM_LIMIT),
        name="out_proj_s",
    )(xs, fps, mps, w_top, w_bot, gain)


def _flat_consts(page, group):
    lane = np.arange(LANES)
    b, h = lane // F_HEADS, lane % F_HEADS
    same = h[:, None] == h[None, :]
    m_blk = (same & (b[:, None] > b[None, :])).astype(np.float32)
    t_blk = same.astype(np.float32)
    r = np.arange(group * (page * F_HEADS // LANES))
    u_row = r[None, :] > r[:, None]
    return (jnp.asarray(m_blk, BF16), jnp.asarray(t_blk, BF16), jnp.asarray(u_row.astype(np.float32), BF16))


def _paged_kernel(pt_ref, q_ref, kn_ref, vn_ref, gs_ref, b_ref, mb_ref, tb_ref, u_ref, ck_hbm, cv_hbm,
                  *rest, group, scale):
    lf_refs = rest[:group]
    o_ref, m_ref, l_ref, acc_ref, base_ref, kbuf, vbuf, sem = rest[group:]
    ng = pl.num_programs(1)
    total = pl.num_programs(0) * ng
    step = pl.program_id(0) * ng + pl.program_id(1)

    def page_copies(slot, pages):
        for j, page in enumerate(pages):
            yield pltpu.make_async_copy(ck_hbm.at[0, page], kbuf.at[slot, j], sem.at[0, slot])
            yield pltpu.make_async_copy(cv_hbm.at[0, page], vbuf.at[slot, j], sem.at[1, slot])

    def fetch(t):
        bb = t // ng
        first = (ng - 1 - t % ng) * group
        for c in page_copies(t % KV_SLOTS, [pt_ref[bb, first + j] for j in range(group)]):
            c.start()

    @pl.when(step == 0)
    def _():
        for t in range(KV_SLOTS - 1):
            pl.when(t < total)(functools.partial(fetch, t))

    @pl.when(step + (KV_SLOTS - 1) < total)
    def _():
        fetch(step + (KV_SLOTS - 1))

    slot = step % KV_SLOTS
    for c in page_copies(slot, [0] * group):
        c.wait()

    g = pl.program_id(1)
    nblk = lf_refs[0].shape[1]
    w = nblk * LANES
    own = (lax.broadcasted_iota(jnp.int32, (F_HEADS, w), 1) % F_HEADS
           == lax.broadcasted_iota(jnp.int32, (F_HEADS, w), 0))

    @pl.when(g == 0)
    def _():
        m_ref[...] = jnp.sum(q_ref[0] * kn_ref[0], axis=1, keepdims=True) * scale
        l_ref[...] = jnp.ones_like(l_ref)
        acc_ref[...] = vn_ref[0]
        cn = _log_sigmoid(gs_ref[0] + b_ref[...])
        head_lane = lax.broadcasted_iota(jnp.int32, (1, LANES), 1) < F_HEADS
        base_ref[...] = _dot3_r(jnp.where(head_lane, cn, 0.0), tb_ref[...])

    xs = jnp.concatenate([r[0] for r in lf_refs], axis=0)
    tot = _dot3_r(xs, tb_ref[...])
    bias = _dot3_r(xs, mb_ref[...]) + _dot3_l(u_ref[...], tot) + base_ref[...]
    base_ref[...] = base_ref[...] + jnp.sum(tot, axis=0, keepdims=True)

    qb = q_ref[0].astype(BF16)
    s = []
    for j in range(group):
        kb = kbuf[slot, j].reshape(w, F_HDIM).astype(BF16)
        bj = jnp.concatenate([bias[j * nblk + k:j * nblk + k + 1, :] for k in range(nblk)], axis=1)
        sj = lax.dot_general(qb, kb, _NT, preferred_element_type=F32) * scale + bj
        s.append(jnp.where(own, sj, -jnp.inf))
    s = jnp.concatenate(s, axis=1)
    m_prev = m_ref[...]
    m_new = jnp.maximum(m_prev, jnp.max(s, axis=1, keepdims=True))
    alpha = jnp.exp(m_prev - m_new)
    p = jnp.exp(s - m_new)
    l_ref[...] = alpha * l_ref[...] + jnp.sum(p, axis=1, keepdims=True)
    pv = None
    for j in range(group):
        pj = p[:, j * w:(j + 1) * w].astype(BF16)
        vb = vbuf[slot, j].reshape(w, F_HDIM).astype(BF16)
        d = jnp.dot(pj, vb, preferred_element_type=F32)
        pv = d if pv is None else pv + d
    acc_ref[...] = alpha * acc_ref[...] + pv
    m_ref[...] = m_new

    @pl.when(g == pl.num_programs(1) - 1)
    def _():
        o_ref[0] = acc_ref[...] / l_ref[...]


def _paged_call(page_table, q8, kn8, vn8, gs3, bias_row, cache_k, cache_v, lf_rows, group):
    b, n_pages = page_table.shape
    page = cache_k.shape[2]
    w = page * F_HEADS
    ng = n_pages // group
    m_blk, t_blk, u_row = _flat_consts(page, group)
    nrow = u_row.shape[0]

    def page_map3(j):
        return lambda bi, g, pt: (pt[bi, (ng - 1 - g) * group + j], 0, 0)

    head = pl.BlockSpec((1, F_HEADS, F_HDIM), lambda bi, g, pt: (bi, 0, 0))
    const = lambda bi, g, pt: (0, 0)
    grid_spec = pltpu.PrefetchScalarGridSpec(
        num_scalar_prefetch=1,
        grid=(b, ng),
        in_specs=(
            [head, head, head,
             pl.BlockSpec((1, 1, LANES), lambda bi, g, pt: (bi, 0, 0)),
             pl.BlockSpec((1, LANES), const),
             pl.BlockSpec((LANES, LANES), const),
             pl.BlockSpec((LANES, LANES), const),
             pl.BlockSpec((nrow, nrow), const),
             pl.BlockSpec(memory_space=pl.ANY),
             pl.BlockSpec(memory_space=pl.ANY)]
            + [pl.BlockSpec((1, w // LANES, LANES), page_map3(j)) for j in range(group)]
        ),
        out_specs=pl.BlockSpec((1, F_HEADS, F_HDIM), lambda bi, g, pt: (bi, 0, 0)),
        scratch_shapes=[
            pltpu.VMEM((F_HEADS, 1), F32),
            pltpu.VMEM((F_HEADS, 1), F32),
            pltpu.VMEM((F_HEADS, F_HDIM), F32),
            pltpu.VMEM((1, LANES), F32),
            pltpu.VMEM((KV_SLOTS, group, page, F_HEADS, F_HDIM), F32),
            pltpu.VMEM((KV_SLOTS, group, page, F_HEADS, F_HDIM), F32),
            pltpu.SemaphoreType.DMA((2, KV_SLOTS)),
        ],
    )
    return pl.pallas_call(
        functools.partial(_paged_kernel, group=group, scale=F_HDIM ** -0.5),
        grid_spec=grid_spec,
        out_shape=jax.ShapeDtypeStruct((b, F_HEADS, F_HDIM), F32),
        compiler_params=_cparams(("arbitrary", "arbitrary")),
        name="paged_attn",
    )(page_table, q8, kn8, vn8, gs3, bias_row, m_blk, t_blk, u_row,
      cache_k, cache_v, *([lf_rows] * group))


def _step_kernel(qk_ref, v_ref, o_ref, z_ref, fz_ref, gs_ref, b_ref, fo_ref, gain_ref,
                 c_ref, n_ref, m_ref,
                 c_out, n_out, m_out, lf_out, fp_out, mp_out, *, sb):
    g = gs_ref[...] + b_ref[...]
    ls = _log_sigmoid(g)
    lf_out[...] = ls[:, G_FOX:G_FOX + F_HEADS]
    fp_out[...] = _silu(fz_ref[...]) * fo_ref[...]
    first_row = lax.broadcasted_iota(jnp.int32, (CHUNK, 1), 0) == 0
    for b in range(sb):
        for h in range(M_HEADS):
            qs = slice(h * M_QKDIM, (h + 1) * M_QKDIM)
            ks = slice(M_QKW + h * M_QKDIM, M_QKW + (h + 1) * M_QKDIM)
            vs = slice(h * M_VDIM, (h + 1) * M_VDIM)
            q = qk_ref[b:b + 1, qs]
            k = qk_ref[b:b + 1, ks] * (M_QKDIM ** -0.5)
            v = v_ref[b:b + 1, vs]
            logi = g[b:b + 1, G_MI + h:G_MI + h + 1]
            logf = ls[b:b + 1, G_MF + h:G_MF + h + 1]
            m_prev = m_ref[b:b + 1, h:h + 1]
            inter = logf + m_prev
            m_t = jnp.maximum(inter, logi)
            dw = jnp.exp(logi - m_t)
            iw = jnp.exp(inter - m_t)
            c_old = c_ref[b, h]
            n_old = n_ref[b, h:h + 1, :]
            s = jnp.sum(q * k, axis=1, keepdims=True) * dw
            cq = lax.dot_general(q.astype(BF16), c_old.astype(BF16), _NT, preferred_element_type=F32)
            num = s * v + iw * cq
            den = s + iw * jnp.sum(q * n_old, axis=1, keepdims=True)
            hh = num / jnp.maximum(jnp.abs(den), jnp.exp(-m_t))
            v_pad = jnp.where(first_row, jnp.broadcast_to(v * dw, (CHUNK, M_VDIM)), 0.0).astype(BF16)
            k_pad = jnp.broadcast_to(k, (CHUNK, M_QKDIM)).astype(BF16)
            c_out[b, h] = iw * c_old + lax.dot_general(v_pad, k_pad, _TN, preferred_element_type=F32)
            n_out[b, h:h + 1, :] = iw * n_old + dw * k
            m_out[b:b + 1, h:h + 1] = m_t
            hn = hh * lax.rsqrt(jnp.mean(hh * hh, axis=1, keepdims=True) + EPS) * gain_ref[:, vs]
            mp_out[b:b + 1, vs] = _silu(z_ref[b:b + 1, vs]) * _sigmoid(o_ref[b:b + 1, vs]) * hn


def _step_call(ps, gs, bias_row, fo, gain, state_c, state_n, state_m, sb):
    b = ps.shape[0]
    col = lambda c: pl.BlockSpec((sb, COLB), lambda i: (i, c))
    const = lambda i: (0, 0)
    return pl.pallas_call(
        functools.partial(_step_kernel, sb=sb),
        grid=(b // sb,),
        in_specs=[
            col(4), col(5), col(6), col(7), col(3),
            pl.BlockSpec((sb, LANES), lambda i: (i, 0)),
            pl.BlockSpec((1, LANES), const),
            pl.BlockSpec((sb, F_WIDTH), lambda i: (i, 0)),
            pl.BlockSpec((1, M_WIDTH), const),
            pl.BlockSpec((sb, M_HEADS, M_VDIM, M_QKDIM), lambda i: (i, 0, 0, 0)),
            pl.BlockSpec((sb, M_HEADS, M_QKDIM), lambda i: (i, 0, 0)),
            pl.BlockSpec((sb, M_HEADS), lambda i: (i, 0)),
        ],
        out_specs=[
            pl.BlockSpec((sb, M_HEADS, M_VDIM, M_QKDIM), lambda i: (i, 0, 0, 0)),
            pl.BlockSpec((sb, M_HEADS, M_QKDIM), lambda i: (i, 0, 0)),
            pl.BlockSpec((sb, M_HEADS), lambda i: (i, 0)),
            pl.BlockSpec((sb, F_HEADS), lambda i: (i, 0)),
            pl.BlockSpec((sb, F_WIDTH), lambda i: (i, 0)),
            pl.BlockSpec((sb, M_WIDTH), lambda i: (i, 0)),
        ],
        out_shape=[
            jax.ShapeDtypeStruct(state_c.shape, F32),
            jax.ShapeDtypeStruct(state_n.shape, F32),
            jax.ShapeDtypeStruct(state_m.shape, F32),
            jax.ShapeDtypeStruct((b, F_HEADS), F32),
            jax.ShapeDtypeStruct((b, F_WIDTH), F32),
            jax.ShapeDtypeStruct((b, M_WIDTH), F32),
        ],
        compiler_params=_cparams(("arbitrary",)),
        name="mlstm_step",
    )(ps, ps, ps, ps, ps, gs, bias_row, fo, gain, state_c, state_n, state_m)


_COL_START = np.concatenate([[0], np.cumsum(PROJ_SIZES)])


def _repack_kernel(wt_ref, wp_ref, wg_ref):
    o = _COL_START
    for n, a in enumerate((o[0], o[1], o[2], o[4], o[5], o[7], o[10], o[11])):
        wp_ref[:, n * COLB:(n + 1) * COLB] = wt_ref[a:a + COLB, :].T.astype(BF16)
    assert o[3] % 8 == 0 and o[8] % 8 == 0 and o[9] == o[8] + M_HEADS and o[10] == o[9] + M_HEADS
    assert (G_FOX, G_MI, G_MF) == (0, F_HEADS, F_HEADS + M_HEADS)
    pad = jnp.zeros((LANES - F_HEADS - 2 * M_HEADS, wt_ref.shape[1]), F32)
    wg = jnp.concatenate([wt_ref[o[3]:o[3] + F_HEADS, :], wt_ref[o[8]:o[8] + 2 * M_HEADS, :], pad], axis=0)
    wg_ref[...] = wg.T.astype(BF16)


def _repack_call(wt):
    dp, d = wt.shape
    tc = _pick_tile(d, (256, 128))
    return pl.pallas_call(
        _repack_kernel,
        grid=(d // tc,),
        in_specs=[pl.BlockSpec((dp, tc), lambda i: (0, i))],
        out_specs=[pl.BlockSpec((tc, N_COLB * COLB), lambda i: (i, 0)),
                   pl.BlockSpec((tc, LANES), lambda i: (i, 0))],
        out_shape=[jax.ShapeDtypeStruct((d, N_COLB * COLB), BF16),
                   jax.ShapeDtypeStruct((d, LANES), BF16)],
        compiler_params=_cparams(("arbitrary",)),
        name="repack",
    )(wt)


def kernel(x_prompt, x_sample, cache_k, cache_v, cache_logf, state_C, state_n, state_m,
           page_table, meta_tokens, norm_gain, w_in, b_fox_f, b_m_i, b_m_f, mh_gain,
           w_out, final_gain):
    batch, seq, d = x_prompt.shape
    depth = w_in.shape[0]
    assert batch == 1 and depth == 1 and x_sample.shape[1] == 1
    assert seq % CHUNK == 0
    s = x_sample.shape[0]
    n_real = N_META + seq
    lp = pl.cdiv(n_real, CHUNK) * CHUNK
    tile = _pick_tile(lp, (640, 512, 384, 256, 128))
    tile_out = _pick_tile(seq, (512, 256, 128))
    n_pool, page = cache_k.shape[1], cache_k.shape[2]
    n_pages = page_table.shape[1]
    assert page == CHUNK and n_pages % 8 == 0 and s % 8 == 0

    w_main, w_gate = _repack_call(w_in[0].T)
    w_o = w_out[0].astype(BF16)
    w_top, w_bot = w_o[:F_WIDTH], w_o[F_WIDTH:]
    bias_row = jnp.concatenate(
        [b_fox_f[0], b_m_i[0], b_m_f[0], jnp.zeros((LANES - F_HEADS - 2 * M_HEADS,), F32)])[None, :]
    gain = norm_gain[0][None, :]
    mh = mh_gain[0][None, :]
    fgain = final_gain[None, :]
    x2 = x_prompt[0]
    xs2 = x_sample[:, 0, :]

    xn, xsn = _norm_call(x2, meta_tokens, gain, xs2, lp, tile)
    qt_bf, k_out, k_bf, v_out, vt_bf, p2, gates, ps, gs = _inproj_call(xn, w_main, w_gate, xsn, n_real, tile)
    gc, ct, lf_p = _gates_call(gates, bias_row, n_real, tile)

    fpart = _fox_call(qt_bf, k_bf, vt_bf, ct, gc, p2, tile)
    mpart, c_p, n_p, m_p = _mlstm_call(p2, gc, ct, mh)
    y_p = _out_call(x2, fpart, mpart, w_top, w_bot, fgain, tile_out)

    hd = (F_HEADS, F_HDIM)
    lf_rows = cache_logf.reshape(depth * n_pool, page * F_HEADS // LANES, LANES)
    fo = _paged_call(page_table,
                     ps[:, 0:COLB].reshape(s, *hd), ps[:, COLB:2 * COLB].reshape(s, *hd),
                     ps[:, 2 * COLB:3 * COLB].reshape(s, *hd), gs.reshape(s, 1, LANES), bias_row,
                     cache_k, cache_v, lf_rows, group=8)
    c_s, n_s, m_s, lf_s, fps, mps = _step_call(ps, gs, bias_row, fo.reshape(s, F_WIDTH), mh,
                                               state_C[0], state_n[0], state_m[0], sb=8)
    y_s = _out_s_call(xs2, fps, mps, w_top, w_bot, fgain)

    return (
        y_p[None],
        y_s[:, None, :],
        k_out.reshape(1, 1, n_real, *hd),
        v_out.reshape(1, 1, n_real, *hd),
        lf_p.reshape(1, 1, n_real, F_HEADS),
        c_p[None, None],
        n_p[:, 0, :][None, None],
        m_p[:, 0, 0][None, None],
        ps[:, COLB:2 * COLB].reshape(1, s, 1, *hd),
        ps[:, 2 * COLB:3 * COLB].reshape(1, s, 1, *hd),
        lf_s.reshape(1, s, 1, F_HEADS),
        c_s[None],
        n_s[None],
        m_s[None],
    )
```

```python
import functools

import numpy as np
import jax
import jax.numpy as jnp
from jax import lax
from jax.experimental import pallas as pl
from jax.experimental.pallas import tpu as pltpu

F32 = jnp.float32
BF16 = jnp.bfloat16

N_META = 16
EPS = 1e-6
F_HEADS = 8
F_HDIM = 128
F_WIDTH = F_HEADS * F_HDIM
M_HEADS = 4
M_VDIM = 256
M_QKDIM = 128
M_WIDTH = M_HEADS * M_VDIM
M_QKW = M_HEADS * M_QKDIM
PROJ_SIZES = (F_WIDTH, F_WIDTH, F_WIDTH, F_HEADS, F_WIDTH,
              M_QKW, M_QKW, M_WIDTH, M_HEADS, M_HEADS, M_WIDTH, M_WIDTH)

LANES = 128
CHUNK = 128
COLB = 1024
MXU_N = 256
N_COLB = 8
G_FOX = 0
G_MI = 8
G_MF = 12
NEG = -1e30
LOG2E = 1.4426950408889634
QSCALE = F_HDIM ** -0.5 * LOG2E
VMEM_LIMIT = 56 * 1024 * 1024
KV_SLOTS = 3

_NT = (((1,), (1,)), ((), ()))
_TN = (((0,), (0,)), ((), ()))


def _cparams(sem, vmem=VMEM_LIMIT):
    return pltpu.CompilerParams(dimension_semantics=sem, vmem_limit_bytes=vmem)


def _pick_tile(n, cands):
    for c in cands:
        if n % c == 0:
            return c
    raise ValueError(f"no tile for {n}")


def _log_sigmoid(x):
    return jnp.minimum(x, 0.0) - jnp.log1p(jnp.exp(-jnp.abs(x)))


def _sigmoid(x):
    return 1.0 / (1.0 + jnp.exp(-x))


def _silu(x):
    return x * _sigmoid(x)


def _split3(x):
    x1 = x.astype(BF16)
    r1 = x - x1.astype(F32)
    x2 = r1.astype(BF16)
    r2 = r1 - x2.astype(F32)
    x3 = r2.astype(BF16)
    return x1, x2, x3


def _dot3_l(a_bf, x):
    x1, x2, x3 = _split3(x)
    d = lambda v: jnp.dot(a_bf, v, preferred_element_type=F32)
    return (d(x3) + d(x2)) + d(x1)


def _dot3_r(x, b_bf):
    x1, x2, x3 = _split3(x)
    d = lambda v: jnp.dot(v, b_bf, preferred_element_type=F32)
    return (d(x3) + d(x2)) + d(x1)


def _norm_kernel(x_ref, prev_ref, meta_ref, g_ref, xs_ref, xn_ref, xsn_ref, *, tile, n_real):
    i = pl.program_id(0)
    g = g_ref[...]

    def nrm(x):
        return x * lax.rsqrt(jnp.mean(x * x, axis=-1, keepdims=True) + EPS) * g

    top = jnp.where(i == 0, meta_ref[...], prev_ref[...])
    xn_ref[0:N_META, :] = nrm(top).astype(BF16)
    body = x_ref[0:tile - N_META, :]
    row = i * tile + N_META + lax.broadcasted_iota(jnp.int32, (tile - N_META, 1), 0)
    xn_ref[N_META:tile, :] = jnp.where(row < n_real, nrm(body), 0.0).astype(BF16)

    @pl.when(i == 0)
    def _():
        xsn_ref[...] = nrm(xs_ref[...]).astype(BF16)


def _norm_call(x, meta, gain, xs, lp, tile):
    seq, d = x.shape
    n_real = seq + N_META
    nblk = pl.cdiv(seq, tile)
    per = tile // N_META
    nprev = seq // N_META
    s = xs.shape[0]
    return pl.pallas_call(
        functools.partial(_norm_kernel, tile=tile, n_real=n_real),
        grid=(lp // tile,),
        in_specs=[
            pl.BlockSpec((tile, d), lambda i: (jnp.minimum(i, nblk - 1), 0)),
            pl.BlockSpec((N_META, d), lambda i: (jnp.clip(i * per - 1, 0, nprev - 1), 0)),
            pl.BlockSpec((N_META, d), lambda i: (0, 0)),
            pl.BlockSpec((1, d), lambda i: (0, 0)),
            pl.BlockSpec((s, d), lambda i: (0, 0)),
        ],
        out_specs=[
            pl.BlockSpec((tile, d), lambda i: (i, 0)),
            pl.BlockSpec((s, d), lambda i: (0, 0)),
        ],
        out_shape=[jax.ShapeDtypeStruct((lp, d), BF16), jax.ShapeDtypeStruct((s, d), BF16)],
        compiler_params=_cparams(("arbitrary",)),
        name="norm",
    )(x, x, meta, gain, xs)


def _inproj_kernel(x_ref, w_ref, wg_ref, xs_ref,
                   q_ref, ko_ref, kb_ref, vo_ref, vb_ref, p2_ref, g_ref, ps_ref, gs_ref):
    n = pl.program_id(0)
    m = pl.program_id(1)

    def column_chunks():
        for c0 in range(0, COLB, MXU_N):
            cs = slice(c0, c0 + MXU_N)
            yield cs, jnp.dot(x_ref[...], w_ref[:, cs], preferred_element_type=F32)

    @pl.when(n == 0)
    def _():
        for cs, acc in column_chunks():
            q_ref[cs, :] = (acc * QSCALE).T.astype(BF16)
        g_ref[...] = jnp.dot(x_ref[...], wg_ref[...], preferred_element_type=F32)

    @pl.when(n == 1)
    def _():
        for cs, acc in column_chunks():
            ko_ref[:, cs] = acc
            kb_ref[:, cs] = acc.astype(BF16)

    @pl.when(n == 2)
    def _():
        for cs, acc in column_chunks():
            vo_ref[:, cs] = acc
            vb_ref[cs, :] = acc.T.astype(BF16)

    @pl.when(n >= 3)
    def _():
        for cs, acc in column_chunks():
            p2_ref[:, cs] = acc

    @pl.when(m == 0)
    def _():
        ps_ref[...] = jnp.dot(xs_ref[...], w_ref[...], preferred_element_type=F32)

    @pl.when((m == 0) & (n == 0))
    def _():
        gs_ref[...] = jnp.dot(xs_ref[...], wg_ref[...], preferred_element_type=F32)


def _inproj_call(xn, w_main, w_gate, xsn, n_real, tile):
    lp, d = xn.shape
    s = xsn.shape[0]
    nm = lp // tile

    def held(n, m, lo, hi):
        return jnp.where(n < lo, 0, jnp.where(n > hi, nm - 1, m))

    def own(lo, hi):
        return lambda n, m: (held(n, m, lo, hi), 0)

    def own_t(lo, hi):
        return lambda n, m: (0, held(n, m, lo, hi))

    return pl.pallas_call(
        _inproj_kernel,
        grid=(N_COLB, nm),
        in_specs=[
            pl.BlockSpec((tile, d), lambda n, m: (m, 0)),
            pl.BlockSpec((d, COLB), lambda n, m: (0, n)),
            pl.BlockSpec((d, LANES), lambda n, m: (0, 0)),
            pl.BlockSpec((s, d), lambda n, m: (0, 0)),
        ],
        out_specs=[
            pl.BlockSpec((COLB, tile), own_t(0, 0)),
            pl.BlockSpec((tile, COLB), own(1, 1)),
            pl.BlockSpec((tile, COLB), own(1, 1)),
            pl.BlockSpec((tile, COLB), own(2, 2)),
            pl.BlockSpec((COLB, tile), own_t(2, 2)),
            pl.BlockSpec((tile, COLB), lambda n, m: (jnp.where(n < 3, 0, m), jnp.maximum(n - 3, 0))),
            pl.BlockSpec((tile, LANES), own(0, 0)),
            pl.BlockSpec((s, COLB), lambda n, m: (0, n)),
            pl.BlockSpec((s, LANES), lambda n, m: (0, 0)),
        ],
        out_shape=[
            jax.ShapeDtypeStruct((COLB, lp), BF16),
            jax.ShapeDtypeStruct((n_real, COLB), F32),
            jax.ShapeDtypeStruct((lp, COLB), BF16),
            jax.ShapeDtypeStruct((n_real, COLB), F32),
            jax.ShapeDtypeStruct((COLB, lp), BF16),
            jax.ShapeDtypeStruct((lp, (N_COLB - 3) * COLB), F32),
            jax.ShapeDtypeStruct((lp, LANES), F32),
            jax.ShapeDtypeStruct((s, N_COLB * COLB), F32),
            jax.ShapeDtypeStruct((s, LANES), F32),
        ],
        compiler_params=_cparams(("arbitrary", "arbitrary")),
        name="in_proj",
    )(xn, w_main, w_gate, xsn)


def _gates_kernel(g_ref, b_ref, ltri_ref, gc_ref, ct_ref, lf_ref, carry_ref, *, n_real):
    i = pl.program_id(0)
    rows = g_ref.shape[0]

    @pl.when(i == 0)
    def _():
        carry_ref[...] = jnp.zeros_like(carry_ref)

    lane = lax.broadcasted_iota(jnp.int32, (CHUNK, LANES), 1)
    is_mf = (lane >= G_MF) & (lane < G_MF + M_HEADS)
    carry = carry_ref[...]
    for u in range(rows // CHUNK):
        rs = slice(u * CHUNK, (u + 1) * CHUNK)
        x = g_ref[rs, :] + b_ref[...]
        row = i * rows + u * CHUNK + lax.broadcasted_iota(jnp.int32, (CHUNK, LANES), 0)
        valid = row < n_real
        ls = _log_sigmoid(x)
        summed = jnp.where(valid & ((lane < F_HEADS) | is_mf), ls, 0.0)
        cs = _dot3_l(ltri_ref[...], summed)
        glob = cs + carry
        out = jnp.where(lane < F_HEADS, glob, jnp.where(is_mf, cs, jnp.where(valid, x, NEG)))
        gc_ref[rs, :] = out
        ct_ref[:, rs] = out.T[0:16, :]
        lf_ref[rs, :] = ls[:, 0:F_HEADS]
        carry = glob[CHUNK - 1:CHUNK, :]
    carry_ref[...] = carry


def _gates_call(gates, bias_row, n_real, tile):
    lp = gates.shape[0]
    ltri = jnp.asarray(np.tril(np.ones((CHUNK, CHUNK), np.float32)), BF16)
    return pl.pallas_call(
        functools.partial(_gates_kernel, n_real=n_real),
        grid=(lp // tile,),
        in_specs=[
            pl.BlockSpec((tile, LANES), lambda i: (i, 0)),
            pl.BlockSpec((1, LANES), lambda i: (0, 0)),
            pl.BlockSpec((CHUNK, CHUNK), lambda i: (0, 0)),
        ],
        out_specs=[
            pl.BlockSpec((tile, LANES), lambda i: (i, 0)),
            pl.BlockSpec((16, tile), lambda i: (0, i)),
            pl.BlockSpec((tile, F_HEADS), lambda i: (i, 0)),
        ],
        out_shape=[
            jax.ShapeDtypeStruct((lp, LANES), F32),
            jax.ShapeDtypeStruct((16, lp), F32),
            jax.ShapeDtypeStruct((n_real, F_HEADS), F32),
        ],
        scratch_shapes=[pltpu.VMEM((1, LANES), F32)],
        compiler_params=_cparams(("arbitrary",)),
        name="gates",
    )(gates, bias_row, ltri)


def _fox_kernel(qi_ref, ki_ref, qt_ref, k_ref, vt_ref, gck_ref, ctq_ref, fz_ref, o_ref,
                acc_ref, m_ref, l_ref, t_ref):
    step_id = pl.program_id(0)
    qi = qi_ref[step_id]
    ki = ki_ref[step_id]
    tk = k_ref.shape[0]
    tq = qt_ref.shape[1]
    cols = [(c0, LANES) for c0 in range(0, tq, LANES)]

    @pl.when(ki == 0)
    def _():
        m_ref[...] = jnp.full_like(m_ref, -jnp.inf)
        l_ref[...] = jnp.zeros_like(l_ref)
        acc_ref[...] = jnp.zeros_like(acc_ref)

    def step(diag):
        for h in range(F_HEADS):
            hs = slice(h * F_HDIM, (h + 1) * F_HDIM)
            ck = gck_ref[:, G_FOX + h:G_FOX + h + 1] * LOG2E
            for c0, cw in cols:
                cs = slice(c0, c0 + cw)
                nk = c0 + cw if diag else tk
                s = jnp.dot(k_ref[0:nk, hs], qt_ref[hs, cs], preferred_element_type=F32)
                t = s - ck[0:nk, :]
                if diag:
                    keep = (lax.broadcasted_iota(jnp.int32, (nk, cw), 0)
                            <= c0 + lax.broadcasted_iota(jnp.int32, (nk, cw), 1))
                    t = jnp.where(keep, t, -jnp.inf)
                t_ref[0:nk, 0:cw] = t
                cq = ctq_ref[G_FOX + h:G_FOX + h + 1, cs] * LOG2E
                m_prev = m_ref[h, :, cs]
                m_new = jnp.maximum(m_prev, jnp.max(t, axis=0, keepdims=True) + cq)
                p = jnp.exp2(t_ref[0:nk, 0:cw] - (m_new - cq))
                alpha = jnp.exp2(m_prev - m_new)
                l_ref[h, :, cs] = alpha * l_ref[h, :, cs] + jnp.sum(p, axis=0, keepdims=True)
                acc_ref[hs, cs] = alpha * acc_ref[hs, cs] + jnp.dot(
                    vt_ref[hs, 0:nk], p.astype(BF16), preferred_element_type=F32)
                m_ref[h, :, cs] = m_new

    @pl.when(ki < qi)
    def _():
        step(False)

    @pl.when(ki == qi)
    def _():
        step(True)
        for h in range(F_HEADS):
            hs = slice(h * F_HDIM, (h + 1) * F_HDIM)
            for c in range(tq // LANES):
                cs = slice(c * LANES, (c + 1) * LANES)
                o = (acc_ref[hs, cs] / l_ref[h, :, cs]).T
                o_ref[cs, hs] = (_silu(fz_ref[cs, hs]) * o).astype(BF16)


def _fox_call(qt, k, vt, ct, gc, p2, tile):
    lp = k.shape[0]
    nq = lp // tile
    pairs = [(q, kk) for q in range(nq) for kk in range(q + 1)]
    qi_list = jnp.asarray(np.array([p[0] for p in pairs], np.int32))
    ki_list = jnp.asarray(np.array([p[1] for p in pairs], np.int32))
    grid_spec = pltpu.PrefetchScalarGridSpec(
        num_scalar_prefetch=2,
        grid=(len(pairs),),
        in_specs=[
            pl.BlockSpec((F_WIDTH, tile), lambda s, qi, ki: (0, qi[s])),
            pl.BlockSpec((tile, F_WIDTH), lambda s, qi, ki: (ki[s], 0)),
            pl.BlockSpec((F_WIDTH, tile), lambda s, qi, ki: (0, ki[s])),
            pl.BlockSpec((tile, LANES), lambda s, qi, ki: (ki[s], 0)),
            pl.BlockSpec((16, tile), lambda s, qi, ki: (0, qi[s])),
            pl.BlockSpec((tile, COLB), lambda s, qi, ki: (qi[s], 0)),
        ],
        out_specs=pl.BlockSpec((tile, F_WIDTH), lambda s, qi, ki: (qi[s], 0)),
        scratch_shapes=[
            pltpu.VMEM((F_WIDTH, tile), F32),
            pltpu.VMEM((F_HEADS, 1, tile), F32),
            pltpu.VMEM((F_HEADS, 1, tile), F32),
            pltpu.VMEM((tile, LANES), F32),
        ],
    )
    return pl.pallas_call(
        _fox_kernel,
        grid_spec=grid_spec,
        out_shape=jax.ShapeDtypeStruct((lp, F_WIDTH), BF16),
        compiler_params=_cparams(("arbitrary",)),
        name="fox",
    )(qi_list, ki_list, qt, k, vt, gc, ct, p2)


def _mlstm_kernel(qk_ref, v_ref, o_ref, z_ref, gc_ref, ct_ref, gain_ref,
                  mp_ref, c_out, n_out, m_out, c_s, n_s, m_s):
    i = pl.program_id(0)

    @pl.when(i == 0)
    def _():
        c_s[...] = jnp.zeros_like(c_s)
        n_s[...] = jnp.zeros_like(n_s)
        m_s[...] = jnp.zeros_like(m_s)

    causal = (lax.broadcasted_iota(jnp.int32, (CHUNK, CHUNK), 0)
              >= lax.broadcasted_iota(jnp.int32, (CHUNK, CHUNK), 1))
    for h in range(M_HEADS):
        qs = slice(h * M_QKDIM, (h + 1) * M_QKDIM)
        ks = slice(M_QKW + h * M_QKDIM, M_QKW + (h + 1) * M_QKDIM)
        vs = slice(h * M_VDIM, (h + 1) * M_VDIM)
        q = qk_ref[:, qs]
        k = qk_ref[:, ks] * (M_QKDIM ** -0.5)
        v = v_ref[:, vs]
        bt_c = gc_ref[:, G_MF + h:G_MF + h + 1]
        it_c = gc_ref[:, G_MI + h:G_MI + h + 1]
        bt_r = ct_ref[G_MF + h:G_MF + h + 1, :]
        it_r = ct_ref[G_MI + h:G_MI + h + 1, :]
        m_prev = m_s[h][:, 0:1]
        dlog = jnp.where(causal, bt_c - bt_r + it_r, -jnp.inf)
        inter = bt_c + m_prev
        m_t = jnp.maximum(inter, jnp.max(dlog, axis=1, keepdims=True))
        dw = jnp.exp(dlog - m_t)
        iw = jnp.exp(inter - m_t)
        qb = q.astype(BF16)
        kb = k.astype(BF16)
        s = lax.dot_general(qb, kb, _NT, preferred_element_type=F32) * dw
        c_old = c_s[h]
        n_old = n_s[h]
        num = (jnp.dot(s.astype(BF16), v.astype(BF16), preferred_element_type=F32)
               + iw * lax.dot_general(qb, c_old.astype(BF16), _NT, preferred_element_type=F32))
        den = jnp.sum(s, axis=1, keepdims=True) + iw * jnp.sum(q * n_old, axis=1, keepdims=True)
        hh = num / jnp.maximum(jnp.abs(den), jnp.exp(-m_t))
        m_new = m_t[CHUNK - 1:CHUNK, :]
        bt_last = bt_c[CHUNK - 1:CHUNK, :]
        wl_c = jnp.exp(bt_last - bt_c + it_c - m_new)
        decay = jnp.exp(bt_last + m_prev - m_new)
        vw = (v * wl_c).astype(BF16)
        c_s[h] = decay * c_old + lax.dot_general(vw, kb, _TN, preferred_element_type=F32)
        n_s[h] = decay * n_old + jnp.sum(wl_c * k, axis=0, keepdims=True)
        m_s[h] = jnp.broadcast_to(m_new, (1, LANES))
        hn = hh * lax.rsqrt(jnp.mean(hh * hh, axis=1, keepdims=True) + EPS) * gain_ref[:, vs]
        mp_ref[:, vs] = (_silu(z_ref[:, vs]) * _sigmoid(o_ref[:, vs]) * hn).astype(BF16)

    @pl.when(i == pl.num_programs(0) - 1)
    def _():
        c_out[...] = c_s[...]
        n_out[...] = n_s[...]
        m_out[...] = m_s[...]


def _mlstm_call(p2, gc, ct, gain):
    lp = p2.shape[0]
    nb = lp // CHUNK
    const3 = lambda i: (0, 0, 0)
    return pl.pallas_call(
        _mlstm_kernel,
        grid=(nb,),
        in_specs=[
            pl.BlockSpec((CHUNK, COLB), lambda i: (i, 1)),
            pl.BlockSpec((CHUNK, COLB), lambda i: (i, 2)),
            pl.BlockSpec((CHUNK, COLB), lambda i: (i, 3)),
            pl.BlockSpec((CHUNK, COLB), lambda i: (i, 4)),
            pl.BlockSpec((CHUNK, LANES), lambda i: (i, 0)),
            pl.BlockSpec((16, CHUNK), lambda i: (0, i)),
            pl.BlockSpec((1, M_WIDTH), lambda i: (0, 0)),
        ],
        out_specs=[
            pl.BlockSpec((CHUNK, M_WIDTH), lambda i: (i, 0)),
            pl.BlockSpec((M_HEADS, M_VDIM, M_QKDIM), const3),
            pl.BlockSpec((M_HEADS, 1, M_QKDIM), const3),
            pl.BlockSpec((M_HEADS, 1, LANES), const3),
        ],
        out_shape=[
            jax.ShapeDtypeStruct((lp, M_WIDTH), BF16),
            jax.ShapeDtypeStruct((M_HEADS, M_VDIM, M_QKDIM), F32),
            jax.ShapeDtypeStruct((M_HEADS, 1, M_QKDIM), F32),
            jax.ShapeDtypeStruct((M_HEADS, 1, LANES), F32),
        ],
        scratch_shapes=[
            pltpu.VMEM((M_HEADS, M_VDIM, M_QKDIM), F32),
            pltpu.VMEM((M_HEADS, 1, M_QKDIM), F32),
            pltpu.VMEM((M_HEADS, 1, LANES), F32),
        ],
        compiler_params=_cparams(("arbitrary",)),
        name="mlstm",
    )(p2, p2, p2, p2, gc, ct, gain)


def _out_kernel(x_ref, fm_ref, fn_ref, mm_ref, mn_ref, wt_ref, wb_ref, g_ref, y_ref):
    t = x_ref.shape[0]
    cat_f = jnp.concatenate([fm_ref[N_META:t, :], fn_ref[...]], axis=0)
    cat_m = jnp.concatenate([mm_ref[N_META:t, :], mn_ref[...]], axis=0)
    y = (x_ref[...]
         + jnp.dot(cat_f, wt_ref[...], preferred_element_type=F32)
         + jnp.dot(cat_m, wb_ref[...], preferred_element_type=F32))
    y_ref[...] = y * lax.rsqrt(jnp.mean(y * y, axis=-1, keepdims=True) + EPS) * g_ref[...]


def _out_call(x, fpart, mpart, w_top, w_bot, gain, tile):
    seq, d = x.shape
    per = tile // N_META
    main = lambda i: (i, 0)
    nxt = lambda i: ((i + 1) * per, 0)
    const = lambda i: (0, 0)
    return pl.pallas_call(
        _out_kernel,
        grid=(seq // tile,),
        in_specs=[
            pl.BlockSpec((tile, d), main),
            pl.BlockSpec((tile, F_WIDTH), main),
            pl.BlockSpec((N_META, F_WIDTH), nxt),
            pl.BlockSpec((tile, M_WIDTH), main),
            pl.BlockSpec((N_META, M_WIDTH), nxt),
            pl.BlockSpec((F_WIDTH, d), const),
            pl.BlockSpec((M_WIDTH, d), const),
            pl.BlockSpec((1, d), const),
        ],
        out_specs=pl.BlockSpec((tile, d), main),
        out_shape=jax.ShapeDtypeStruct((seq, d), F32),
        compiler_params=_cparams(("arbitrary",)),
        name="out_proj",
    )(x, fpart, fpart, mpart, mpart, w_top, w_bot, gain)


def _out_s_kernel(x_ref, f_ref, m_ref, wt_ref, wb_ref, g_ref, y_ref):
    y = (x_ref[...]
         + jnp.dot(f_ref[...].astype(BF16), wt_ref[...], preferred_element_type=F32)
         + jnp.dot(m_ref[...].astype(BF16), wb_ref[...], preferred_element_type=F32))
    y_ref[...] = y * lax.rsqrt(jnp.mean(y * y, axis=-1, keepdims=True) + EPS) * g_ref[...]


def _out_s_call(xs, fps, mps, w_top, w_bot, gain):
    s, d = xs.shape
    return pl.pallas_call(
        _out_s_kernel,
        out_shape=jax.ShapeDtypeStruct((s, d), F32),
        compiler_params=pltpu.CompilerParams(vmem_limit_bytes=VMEM_LIMIT),
        name="out_proj_s",
    )(xs, fps, mps, w_top, w_bot, gain)


def _flat_consts(page, group):
    lane = np.arange(LANES)
    b, h = lane // F_HEADS, lane % F_HEADS
    same = h[:, None] == h[None, :]
    m_blk = (same & (b[:, None] > b[None, :])).astype(np.float32)
    t_blk = same.astype(np.float32)
    r = np.arange(group * (page * F_HEADS // LANES))
    u_row = r[None, :] > r[:, None]
    return (jnp.asarray(m_blk, BF16), jnp.asarray(t_blk, BF16), jnp.asarray(u_row.astype(np.float32), BF16))


def _paged_kernel(pt_ref, q_ref, kn_ref, vn_ref, gs_ref, b_ref, mb_ref, tb_ref, u_ref, ck_hbm, cv_hbm,
                  *rest, group, scale):
    lf_refs = rest[:group]
    o_ref, m_ref, l_ref, acc_ref, base_ref, kbuf, vbuf, sem = rest[group:]
    ng = pl.num_programs(1)
    total = pl.num_programs(0) * ng
    step = pl.program_id(0) * ng + pl.program_id(1)

    def page_copies(slot, pages):
        for j, page in enumerate(pages):
            yield pltpu.make_async_copy(ck_hbm.at[0, page], kbuf.at[slot, j], sem.at[0, slot])
            yield pltpu.make_async_copy(cv_hbm.at[0, page], vbuf.at[slot, j], sem.at[1, slot])

    def fetch(t):
        bb = t // ng
        first = (ng - 1 - t % ng) * group
        for c in page_copies(t % KV_SLOTS, [pt_ref[bb, first + j] for j in range(group)]):
            c.start()

    @pl.when(step == 0)
    def _():
        for t in range(KV_SLOTS - 1):
            pl.when(t < total)(functools.partial(fetch, t))

    @pl.when(step + (KV_SLOTS - 1) < total)
    def _():
        fetch(step + (KV_SLOTS - 1))

    slot = step % KV_SLOTS
    for c in page_copies(slot, [0] * group):
        c.wait()

    g = pl.program_id(1)
    nblk = lf_refs[0].shape[1]
    w = nblk * LANES
    own = (lax.broadcasted_iota(jnp.int32, (F_HEADS, w), 1) % F_HEADS
           == lax.broadcasted_iota(jnp.int32, (F_HEADS, w), 0))

    @pl.when(g == 0)
    def _():
        m_ref[...] = jnp.sum(q_ref[0] * kn_ref[0], axis=1, keepdims=True) * scale
        l_ref[...] = jnp.ones_like(l_ref)
        acc_ref[...] = vn_ref[0]
        cn = _log_sigmoid(gs_ref[0] + b_ref[...])
        head_lane = lax.broadcasted_iota(jnp.int32, (1, LANES), 1) < F_HEADS
        base_ref[...] = _dot3_r(jnp.where(head_lane, cn, 0.0), tb_ref[...])

    xs = jnp.concatenate([r[0] for r in lf_refs], axis=0)
    tot = _dot3_r(xs, tb_ref[...])
    bias = _dot3_r(xs, mb_ref[...]) + _dot3_l(u_ref[...], tot) + base_ref[...]
    base_ref[...] = base_ref[...] + jnp.sum(tot, axis=0, keepdims=True)

    qb = q_ref[0].astype(BF16)
    s = []
    for j in range(group):
        kb = kbuf[slot, j].reshape(w, F_HDIM).astype(BF16)
        bj = jnp.concatenate([bias[j * nblk + k:j * nblk + k + 1, :] for k in range(nblk)], axis=1)
        sj = lax.dot_general(qb, kb, _NT, preferred_element_type=F32) * scale + bj
        s.append(jnp.where(own, sj, -jnp.inf))
    s = jnp.concatenate(s, axis=1)
    m_prev = m_ref[...]
    m_new = jnp.maximum(m_prev, jnp.max(s, axis=1, keepdims=True))
    alpha = jnp.exp(m_prev - m_new)
    p = jnp.exp(s - m_new)
    l_ref[...] = alpha * l_ref[...] + jnp.sum(p, axis=1, keepdims=True)
    pv = None
    for j in range(group):
        pj = p[:, j * w:(j + 1) * w].astype(BF16)
        vb = vbuf[slot, j].reshape(w, F_HDIM).astype(BF16)
        d = jnp.dot(pj, vb, preferred_element_type=F32)
        pv = d if pv is None else pv + d
    acc_ref[...] = alpha * acc_ref[...] + pv
    m_ref[...] = m_new

    @pl.when(g == pl.num_programs(1) - 1)
    def _():
        o_ref[0] = acc_ref[...] / l_ref[...]


def _paged_call(page_table, q8, kn8, vn8, gs3, bias_row, cache_k, cache_v, lf_rows, group):
    b, n_pages = page_table.shape
    page = cache_k.shape[2]
    w = page * F_HEADS
    ng = n_pages // group
    m_blk, t_blk, u_row = _flat_consts(page, group)
    nrow = u_row.shape[0]

    def page_map3(j):
        return lambda bi, g, pt: (pt[bi, (ng - 1 - g) * group + j], 0, 0)

    head = pl.BlockSpec((1, F_HEADS, F_HDIM), lambda bi, g, pt: (bi, 0, 0))
    const = lambda bi, g, pt: (0, 0)
    grid_spec = pltpu.PrefetchScalarGridSpec(
        num_scalar_prefetch=1,
        grid=(b, ng),
        in_specs=(
            [head, head, head,
             pl.BlockSpec((1, 1, LANES), lambda bi, g, pt: (bi, 0, 0)),
             pl.BlockSpec((1, LANES), const),
             pl.BlockSpec((LANES, LANES), const),
             pl.BlockSpec((LANES, LANES), const),
             pl.BlockSpec((nrow, nrow), const),
             pl.BlockSpec(memory_space=pl.ANY),
             pl.BlockSpec(memory_space=pl.ANY)]
            + [pl.BlockSpec((1, w // LANES, LANES), page_map3(j)) for j in range(group)]
        ),
        out_specs=pl.BlockSpec((1, F_HEADS, F_HDIM), lambda bi, g, pt: (bi, 0, 0)),
        scratch_shapes=[
            pltpu.VMEM((F_HEADS, 1), F32),
            pltpu.VMEM((F_HEADS, 1), F32),
            pltpu.VMEM((F_HEADS, F_HDIM), F32),
            pltpu.VMEM((1, LANES), F32),
            pltpu.VMEM((KV_SLOTS, group, page, F_HEADS, F_HDIM), F32),
            pltpu.VMEM((KV_SLOTS, group, page, F_HEADS, F_HDIM), F32),
            pltpu.SemaphoreType.DMA((2, KV_SLOTS)),
        ],
    )
    return pl.pallas_call(
        functools.partial(_paged_kernel, group=group, scale=F_HDIM ** -0.5),
        grid_spec=grid_spec,
        out_shape=jax.ShapeDtypeStruct((b, F_HEADS, F_HDIM), F32),
        compiler_params=_cparams(("arbitrary", "arbitrary")),
        name="paged_attn",
    )(page_table, q8, kn8, vn8, gs3, bias_row, m_blk, t_blk, u_row,
      cache_k, cache_v, *([lf_rows] * group))


def _lf_rows_kernel(x_ref, o_ref):
    o_ref[...] = x_ref[...].T


def _lf_rows_call(lf_t):
    w, n_pool = lf_t.shape
    tp = _pick_tile(n_pool, (640, 512, 256, 128, n_pool))
    return pl.pallas_call(
        _lf_rows_kernel,
        grid=(n_pool // tp,),
        in_specs=[pl.BlockSpec((w, tp), lambda i: (0, i))],
        out_specs=pl.BlockSpec((tp, w), lambda i: (i, 0)),
        out_shape=jax.ShapeDtypeStruct((n_pool, w), F32),
        compiler_params=_cparams(("arbitrary",)),
        name="lf_rows",
    )(lf_t)


def _step_kernel(qk_ref, v_ref, o_ref, z_ref, fz_ref, gs_ref, b_ref, fo_ref, gain_ref,
                 c_ref, n_ref, m_ref,
                 c_out, n_out, m_out, lf_out, fp_out, mp_out, *, sb):
    g = gs_ref[...] + b_ref[...]
    ls = _log_sigmoid(g)
    lf_out[...] = ls[:, G_FOX:G_FOX + F_HEADS]
    fp_out[...] = _silu(fz_ref[...]) * fo_ref[...]
    first_row = lax.broadcasted_iota(jnp.int32, (CHUNK, 1), 0) == 0
    for b in range(sb):
        for h in range(M_HEADS):
            qs = slice(h * M_QKDIM, (h + 1) * M_QKDIM)
            ks = slice(M_QKW + h * M_QKDIM, M_QKW + (h + 1) * M_QKDIM)
            vs = slice(h * M_VDIM, (h + 1) * M_VDIM)
            q = qk_ref[b:b + 1, qs]
            k = qk_ref[b:b + 1, ks] * (M_QKDIM ** -0.5)
            v = v_ref[b:b + 1, vs]
            logi = g[b:b + 1, G_MI + h:G_MI + h + 1]
            logf = ls[b:b + 1, G_MF + h:G_MF + h + 1]
            m_prev = m_ref[b:b + 1, h:h + 1]
            inter = logf + m_prev
            m_t = jnp.maximum(inter, logi)
            dw = jnp.exp(logi - m_t)
            iw = jnp.exp(inter - m_t)
            c_old = c_ref[b, h]
            n_old = n_ref[b, h:h + 1, :]
            s = jnp.sum(q * k, axis=1, keepdims=True) * dw
            cq = lax.dot_general(q.astype(BF16), c_old.astype(BF16), _NT, preferred_element_type=F32)
            num = s * v + iw * cq
            den = s + iw * jnp.sum(q * n_old, axis=1, keepdims=True)
            hh = num / jnp.maximum(jnp.abs(den), jnp.exp(-m_t))
            v_pad = jnp.where(first_row, jnp.broadcast_to(v * dw, (CHUNK, M_VDIM)), 0.0).astype(BF16)
            k_pad = jnp.broadcast_to(k, (CHUNK, M_QKDIM)).astype(BF16)
            c_out[b, h] = iw * c_old + lax.dot_general(v_pad, k_pad, _TN, preferred_element_type=F32)
            n_out[b, h:h + 1, :] = iw * n_old + dw * k
            m_out[b:b + 1, h:h + 1] = m_t
            hn = hh * lax.rsqrt(jnp.mean(hh * hh, axis=1, keepdims=True) + EPS) * gain_ref[:, vs]
            mp_out[b:b + 1, vs] = _silu(z_ref[b:b + 1, vs]) * _sigmoid(o_ref[b:b + 1, vs]) * hn


def _step_call(ps, gs, bias_row, fo, gain, state_c, state_n, state_m, sb):
    b = ps.shape[0]
    col = lambda c: pl.BlockSpec((sb, COLB), lambda i: (i, c))
    const = lambda i: (0, 0)
    return pl.pallas_call(
        functools.partial(_step_kernel, sb=sb),
        grid=(b // sb,),
        in_specs=[
            col(4), col(5), col(6), col(7), col(3),
            pl.BlockSpec((sb, LANES), lambda i: (i, 0)),
            pl.BlockSpec((1, LANES), const),
            pl.BlockSpec((sb, F_WIDTH), lambda i: (i, 0)),
            pl.BlockSpec((1, M_WIDTH), const),
            pl.BlockSpec((sb, M_HEADS, M_VDIM, M_QKDIM), lambda i: (i, 0, 0, 0)),
            pl.BlockSpec((sb, M_HEADS, M_QKDIM), lambda i: (i, 0, 0)),
            pl.BlockSpec((sb, M_HEADS), lambda i: (i, 0)),
        ],
        out_specs=[
            pl.BlockSpec((sb, M_HEADS, M_VDIM, M_QKDIM), lambda i: (i, 0, 0, 0)),
            pl.BlockSpec((sb, M_HEADS, M_QKDIM), lambda i: (i, 0, 0)),
            pl.BlockSpec((sb, M_HEADS), lambda i: (i, 0)),
            pl.BlockSpec((sb, F_HEADS), lambda i: (i, 0)),
            pl.BlockSpec((sb, F_WIDTH), lambda i: (i, 0)),
            pl.BlockSpec((sb, M_WIDTH), lambda i: (i, 0)),
        ],
        out_shape=[
            jax.ShapeDtypeStruct(state_c.shape, F32),
            jax.ShapeDtypeStruct(state_n.shape, F32),
            jax.ShapeDtypeStruct(state_m.shape, F32),
            jax.ShapeDtypeStruct((b, F_HEADS), F32),
            jax.ShapeDtypeStruct((b, F_WIDTH), F32),
            jax.ShapeDtypeStruct((b, M_WIDTH), F32),
        ],
        compiler_params=_cparams(("arbitrary",)),
        name="mlstm_step",
    )(ps, ps, ps, ps, ps, gs, bias_row, fo, gain, state_c, state_n, state_m)


_COL_START = np.concatenate([[0], np.cumsum(PROJ_SIZES)])


def _repack_kernel(wt_ref, wp_ref, wg_ref):
    o = _COL_START
    for n, a in enumerate((o[0], o[1], o[2], o[4], o[5], o[7], o[10], o[11])):
        wp_ref[:, n * COLB:(n + 1) * COLB] = wt_ref[a:a + COLB, :].T.astype(BF16)
    assert o[3] % 8 == 0 and o[8] % 8 == 0 and o[9] == o[8] + M_HEADS and o[10] == o[9] + M_HEADS
    assert (G_FOX, G_MI, G_MF) == (0, F_HEADS, F_HEADS + M_HEADS)
    pad = jnp.zeros((LANES - F_HEADS - 2 * M_HEADS, wt_ref.shape[1]), F32)
    wg = jnp.concatenate([wt_ref[o[3]:o[3] + F_HEADS, :], wt_ref[o[8]:o[8] + 2 * M_HEADS, :], pad], axis=0)
    wg_ref[...] = wg.T.astype(BF16)


def _repack_call(wt):
    dp, d = wt.shape
    tc = _pick_tile(d, (256, 128))
    return pl.pallas_call(
        _repack_kernel,
        grid=(d // tc,),
        in_specs=[pl.BlockSpec((dp, tc), lambda i: (0, i))],
        out_specs=[pl.BlockSpec((tc, N_COLB * COLB), lambda i: (i, 0)),
                   pl.BlockSpec((tc, LANES), lambda i: (i, 0))],
        out_shape=[jax.ShapeDtypeStruct((d, N_COLB * COLB), BF16),
                   jax.ShapeDtypeStruct((d, LANES), BF16)],
        compiler_params=_cparams(("arbitrary",)),
        name="repack",
    )(wt)


def kernel(x_prompt, x_sample, cache_k, cache_v, cache_logf, state_C, state_n, state_m,
           page_table, meta_tokens, norm_gain, w_in, b_fox_f, b_m_i, b_m_f, mh_gain,
           w_out, final_gain):
    batch, seq, d = x_prompt.shape
    depth = w_in.shape[0]
    assert batch == 1 and depth == 1 and x_sample.shape[1] == 1
    assert seq % CHUNK == 0
    s = x_sample.shape[0]
    n_real = N_META + seq
    lp = pl.cdiv(n_real, CHUNK) * CHUNK
    tile = _pick_tile(lp, (640, 512, 384, 256, 128))
    tile_out = _pick_tile(seq, (512, 256, 128))
    n_pool, page = cache_k.shape[1], cache_k.shape[2]
    n_pages = page_table.shape[1]
    assert page == CHUNK and n_pages % 8 == 0 and s % 8 == 0

    w_main, w_gate = _repack_call(w_in[0].T)
    w_o = w_out[0].astype(BF16)
    w_top, w_bot = w_o[:F_WIDTH], w_o[F_WIDTH:]
    bias_row = jnp.concatenate(
        [b_fox_f[0], b_m_i[0], b_m_f[0], jnp.zeros((LANES - F_HEADS - 2 * M_HEADS,), F32)])[None, :]
    gain = norm_gain[0][None, :]
    mh = mh_gain[0][None, :]
    fgain = final_gain[None, :]
    x2 = x_prompt[0]
    xs2 = x_sample[:, 0, :]

    xn, xsn = _norm_call(x2, meta_tokens, gain, xs2, lp, tile)
    qt_bf, k_out, k_bf, v_out, vt_bf, p2, gates, ps, gs = _inproj_call(xn, w_main, w_gate, xsn, n_real, tile)
    gc, ct, lf_p = _gates_call(gates, bias_row, n_real, tile)

    fpart = _fox_call(qt_bf, k_bf, vt_bf, ct, gc, p2, tile)
    mpart, c_p, n_p, m_p = _mlstm_call(p2, gc, ct, mh)
    y_p = _out_call(x2, fpart, mpart, w_top, w_bot, fgain, tile_out)

    hd = (F_HEADS, F_HDIM)
    lf_t = jnp.transpose(cache_logf[0], (1, 2, 0)).reshape(page * F_HEADS, n_pool)
    lf_rows = _lf_rows_call(lf_t).reshape(n_pool, page * F_HEADS // LANES, LANES)
    fo = _paged_call(page_table,
                     ps[:, 0:COLB].reshape(s, *hd), ps[:, COLB:2 * COLB].reshape(s, *hd),
                     ps[:, 2 * COLB:3 * COLB].reshape(s, *hd), gs.reshape(s, 1, LANES), bias_row,
                     cache_k, cache_v, lf_rows, group=8)
    c_s, n_s, m_s, lf_s, fps, mps = _step_call(ps, gs, bias_row, fo.reshape(s, F_WIDTH), mh,
                                               state_C[0], state_n[0], state_m[0], sb=8)
    y_s = _out_s_call(xs2, fps, mps, w_top, w_bot, fgain)

    return (
        y_p[None],
        y_s[:, None, :],
        k_out.reshape(1, 1, n_real, *hd),
        v_out.reshape(1, 1, n_real, *hd),
        lf_p.reshape(1, 1, n_real, F_HEADS),
        c_p[None, None],
        n_p[:, 0, :][None, None],
        m_p[:, 0, 0][None, None],
        ps[:, COLB:2 * COLB].reshape(1, s, 1, *hd),
        ps[:, 2 * COLB:3 * COLB].reshape(1, s, 1, *hd),
        lf_s.reshape(1, s, 1, F_HEADS),
        c_s[None],
        n_s[None],
        m_s[None],
    )
```
